```python
import math
import jax, jax.numpy as jnp
from jax import lax
import numpy as np

D_MODEL = 4096
BATCH = 2
SEQ = 8192
DEPTH = 1

N_ATTN_HEADS = 8
HEAD_DIM = 128
ATTN_QK_WIDTH = N_ATTN_HEADS * 2 * HEAD_DIM
ATTN_V_WIDTH = N_ATTN_HEADS * 2 * HEAD_DIM
Q_BLOCK = 128
CONV_WIDTH = 2048
CONV_GROUPS = 16
CONV_K = 3
D_FF = 11008
FFN_CONV_K = 3
LN_EPS = 1e-5
RMS_EPS = 1e-5
ALPHA = (2.0 * DEPTH) ** 0.25
BETA = (8.0 * DEPTH) ** -0.25
IN_SPLIT_SIZES = (ATTN_QK_WIDTH, ATTN_QK_WIDTH, ATTN_V_WIDTH,
                  CONV_WIDTH, CONV_WIDTH, CONV_WIDTH, D_MODEL, D_MODEL)
IN_WIDTH = sum(IN_SPLIT_SIZES)
IN_OFFSETS = tuple(int(o) for o in np.cumsum(IN_SPLIT_SIZES)[:-1])

kernel_name = "hybrid_diffattn_shortconv_convffn_deepnorm"


def lambda_init(layer_idx):
    return 0.8 - 0.6 * math.exp(-0.3 * layer_idx)


def layer_norm(x, g, b):
    xf = x.astype(jnp.float32)
    mu = jnp.mean(xf, axis=-1, keepdims=True)
    xc = xf - mu
    var = jnp.mean(xc * xc, axis=-1, keepdims=True)
    return (xc * lax.rsqrt(var + LN_EPS) * g.astype(jnp.float32) + b.astype(jnp.float32)).astype(x.dtype)


def rms_norm(x, g):
    xf = x.astype(jnp.float32)
    ms = jnp.mean(xf * xf, axis=-1, keepdims=True)
    return (xf * lax.rsqrt(ms + RMS_EPS) * g.astype(jnp.float32)).astype(x.dtype)


def causal_dwconv(u, w):
    k_width = w.shape[0]
    s = u.shape[1]
    up = jnp.pad(u, ((0, 0), (k_width - 1, 0), (0, 0)))
    y = w[0] * up[:, 0:s]
    for j in range(1, k_width):
        y = y + w[j] * up[:, j:j + s]
    return y


def diff_attention(q, k, v, lam):
    b, s, h, _, d = q.shape
    scale = 1.0 / math.sqrt(d)
    n_blk = s // Q_BLOCK
    qt = jnp.transpose(q, (0, 2, 3, 1, 4))
    kt = jnp.transpose(k, (0, 2, 3, 1, 4))
    vt = jnp.transpose(v, (0, 2, 1, 3))
    qb = qt.reshape(b, h, 2, n_blk, Q_BLOCK, d)
    qb = jnp.moveaxis(qb, 3, 0)
    key_pos = jnp.arange(s)
    neg = jnp.finfo(jnp.float32).min

    def one_block(args):
        q_blk, i = args
        sc = jnp.einsum('bhcqd,bhckd->bhcqk', q_blk, kt).astype(jnp.float32) * scale
        q_pos = i * Q_BLOCK + jnp.arange(Q_BLOCK)
        mask = key_pos[None, :] <= q_pos[:, None]
        sc = jnp.where(mask, sc, neg)
        p = jax.nn.softmax(sc, axis=-1)
        a = p[:, :, 0] - lam.astype(jnp.float32) * p[:, :, 1]
        return jnp.einsum('bhqk,bhkv->bhqv', a.astype(vt.dtype), vt)

    out = lax.map(one_block, (qb, jnp.arange(n_blk)))
    out = jnp.transpose(out, (1, 0, 3, 2, 4))
    return out.reshape(b, s, h, 2 * d)


def setup_inputs(seed: int = 0) -> dict:
    key = jax.random.key(seed)
    ks = jax.random.split(key, 24)
    f32 = jnp.float32
    s_in = D_MODEL ** -0.5

    def nrm(k, shape, scale):
        return jax.random.normal(k, shape, f32) * scale

    x = jax.random.normal(ks[0], (BATCH, SEQ, D_MODEL), f32)
    pieces = [
        nrm(ks[1], (DEPTH, D_MODEL, ATTN_QK_WIDTH), s_in),
        nrm(ks[2], (DEPTH, D_MODEL, ATTN_QK_WIDTH), s_in),
        nrm(ks[3], (DEPTH, D_MODEL, ATTN_V_WIDTH), s_in * BETA),
        nrm(ks[4], (DEPTH, D_MODEL, CONV_WIDTH), s_in * BETA),
        nrm(ks[5], (DEPTH, D_MODEL, CONV_WIDTH), s_in),
        nrm(ks[6], (DEPTH, D_MODEL, CONV_WIDTH), s_in),
        nrm(ks[7], (DEPTH, D_MODEL, D_MODEL), s_in),
        nrm(ks[8], (DEPTH, D_MODEL, D_MODEL), s_in),
    ]
    w_in = jnp.concatenate(pieces, axis=-1)
    return {
        "x": x,
        "w_in": w_in,
        "lambda_q1": nrm(ks[9], (DEPTH, HEAD_DIM), 0.1),
        "lambda_k1": nrm(ks[10], (DEPTH, HEAD_DIM), 0.1),
        "lambda_q2": nrm(ks[11], (DEPTH, HEAD_DIM), 0.1),
        "lambda_k2": nrm(ks[12], (DEPTH, HEAD_DIM), 0.1),
        "subln_g": 1.0 + nrm(ks[13], (DEPTH, 2 * HEAD_DIM), 0.02),
        "conv_mix_w": nrm(ks[14], (DEPTH, CONV_K, CONV_WIDTH), CONV_K ** -0.5),
        "w_attn_out": nrm(ks[15], (DEPTH, ATTN_V_WIDTH, D_MODEL), ATTN_V_WIDTH ** -0.5 * BETA),
        "w_conv_out": nrm(ks[16], (DEPTH, CONV_WIDTH, D_MODEL), CONV_WIDTH ** -0.5 * BETA),
        "w_o": nrm(ks[17], (DEPTH, D_MODEL, D_MODEL), s_in * BETA),
        "ln1_g": 1.0 + nrm(ks[18], (DEPTH, D_MODEL), 0.02),
        "ln1_b": nrm(ks[19], (DEPTH, D_MODEL), 0.02),
        "w_up": nrm(ks[20], (DEPTH, D_MODEL, 2 * D_FF), s_in * BETA),
        "ffn_conv_w": nrm(ks[21], (DEPTH, FFN_CONV_K, 2 * D_FF), FFN_CONV_K ** -0.5),
        "w_down": nrm(ks[22], (DEPTH, D_FF, D_MODEL), D_FF ** -0.5 * BETA),
        "ln2_g": 1.0 + nrm(ks[23], (DEPTH, D_MODEL), 0.02),
        "ln2_b": nrm(jax.random.fold_in(ks[23], 1), (DEPTH, D_MODEL), 0.02),
    }


def reference(x, w_in, lambda_q1, lambda_k1, lambda_q2, lambda_k2, subln_g,
              conv_mix_w, w_attn_out, w_conv_out, w_o, ln1_g, ln1_b,
              w_up, ffn_conv_w, w_down, ln2_g, ln2_b):
    b, s, _ = x.shape
    h = x
    for l in range(DEPTH):
        lam_init = lambda_init(l)
        proj = jnp.einsum('bsd,de->bse', h, w_in[l])
        q, k, v, u, g_b, g_c, gate_a, gate_c = jnp.split(proj, IN_OFFSETS, axis=-1)

        q = q.reshape(b, s, N_ATTN_HEADS, 2, HEAD_DIM)
        k = k.reshape(b, s, N_ATTN_HEADS, 2, HEAD_DIM)
        v = v.reshape(b, s, N_ATTN_HEADS, 2 * HEAD_DIM)
        lam = (jnp.exp(jnp.sum(lambda_q1[l].astype(jnp.float32) * lambda_k1[l].astype(jnp.float32)))
               - jnp.exp(jnp.sum(lambda_q2[l].astype(jnp.float32) * lambda_k2[l].astype(jnp.float32)))
               + lam_init)
        attn = diff_attention(q, k, v, lam)
        attn = rms_norm(attn, subln_g[l]) * (1.0 - lam_init)
        y_a = jnp.einsum('bse,ed->bsd', attn.reshape(b, s, ATTN_V_WIDTH), w_attn_out[l])

        yc = g_b * causal_dwconv(g_c * u, conv_mix_w[l])
        y_c = jnp.einsum('bse,ed->bsd', yc, w_conv_out[l])

        merged = jax.nn.sigmoid(gate_a) * y_a + jax.nn.sigmoid(gate_c) * y_c
        mix = jnp.einsum('bsd,de->bse', merged, w_o[l])
        h = layer_norm(ALPHA * h + mix, ln1_g[l], ln1_b[l])

        z = jnp.einsum('bsd,df->bsf', h, w_up[l])
        z = causal_dwconv(z, ffn_conv_w[l])
        z_gate, z_val = jnp.split(z, 2, axis=-1)
        f = jnp.einsum('bsf,fd->bsd', jax.nn.silu(z_gate) * z_val, w_down[l])
        h = layer_norm(ALPHA * h + f, ln2_g[l], ln2_b[l])
    return h
```

```python
import functools
import math

import jax
import jax.numpy as jnp
from jax import lax
from jax.experimental import pallas as pl
from jax.experimental.pallas import tpu as pltpu

LN_EPS = 1e-5
RMS_EPS = 1e-5
LOG2E = 1.4426950408889634
CONV_TAPS = 3
SUBLANES = 8
MASK_VALUE = -1e30
VMEM_LIMIT_BYTES = 56 * 1024 * 1024

F32 = jnp.float32
BF16 = jnp.bfloat16


def _params(n_axes):
    return pltpu.CompilerParams(
        dimension_semantics=("arbitrary",) * n_axes,
        vmem_limit_bytes=VMEM_LIMIT_BYTES,
    )


def _block(dim, want):
    if dim <= want:
        return dim
    b = want
    while dim % b:
        b //= 2
    return b


def _dot(a, b):
    return jnp.dot(a, b, preferred_element_type=F32)


def _proj_scale_kernel(x_ref, w_ref, s_ref, o_ref):
    o_ref[...] = (_dot(x_ref[...], w_ref[...]) * s_ref[...]).astype(o_ref.dtype)


def _proj_scale(x, w, col_scale, *, bm, bn, name):
    m, k = x.shape
    n = w.shape[1]
    bm, bn = _block(m, bm), _block(n, bn)
    return pl.pallas_call(
        _proj_scale_kernel,
        grid=(m // bm, n // bn),
        in_specs=[
            pl.BlockSpec((bm, k), lambda i, j: (i, 0)),
            pl.BlockSpec((k, bn), lambda i, j: (0, j)),
            pl.BlockSpec((1, bn), lambda i, j: (0, j)),
        ],
        out_specs=pl.BlockSpec((bm, bn), lambda i, j: (i, j)),
        out_shape=jax.ShapeDtypeStruct((m, n), BF16),
        compiler_params=_params(2),
        name=name,
    )(x, w, col_scale)


def _proj_residual_kernel(x_ref, w_ref, r_ref, o_ref, *, alpha):
    o_ref[...] = alpha * r_ref[...] + _dot(x_ref[...], w_ref[...])


def _proj_residual(x, w, resid, alpha, *, bm, bn, name):
    m, k = x.shape
    n = w.shape[1]
    bm, bn = _block(m, bm), _block(n, bn)
    return pl.pallas_call(
        functools.partial(_proj_residual_kernel, alpha=alpha),
        grid=(m // bm, n // bn),
        in_specs=[
            pl.BlockSpec((bm, k), lambda i, j: (i, 0)),
            pl.BlockSpec((k, bn), lambda i, j: (0, j)),
            pl.BlockSpec((bm, bn), lambda i, j: (i, j)),
        ],
        out_specs=pl.BlockSpec((bm, bn), lambda i, j: (i, j)),
        out_shape=jax.ShapeDtypeStruct((m, n), F32),
        compiler_params=_params(2),
        name=name,
    )(x, w, resid)


def _conv_body(z, w_ref):
    z1 = pltpu.roll(z, 1, 0)
    z2 = pltpu.roll(z, 2, 0)
    return w_ref[0:1, :] * z2 + w_ref[1:2, :] * z1 + w_ref[2:3, :] * z


def _conv_head(z_head, prev, w_ref):
    row = lax.broadcasted_iota(jnp.int32, z_head.shape, 0)
    z1 = jnp.where(row == 0, prev[SUBLANES - 1:SUBLANES, :], pltpu.roll(z_head, 1, 0))
    z2 = jnp.where(row == 0, prev[SUBLANES - 2:SUBLANES - 1, :],
                   jnp.where(row == 1, prev[SUBLANES - 1:SUBLANES, :], pltpu.roll(z_head, 2, 0)))
    return w_ref[0:1, :] * z2 + w_ref[1:2, :] * z1 + w_ref[2:3, :] * z_head


def _halo(carry_ref, idx, new_tail, seq_start):
    @pl.when(seq_start)
    def _():
        for i in idx[1:]:
            carry_ref[idx[0], i] = jnp.zeros(carry_ref.shape[2:], F32)

    prev = [carry_ref[idx[0], i] for i in idx[1:]]
    for i, t in zip(idx[1:], new_tail):
        carry_ref[idx[0], i] = t
    return prev


def _shortconv_kernel(x_ref, wu_ref, wb_ref, wc_ref, cw_ref, o_ref, carry_ref, *, blocks_per_seq):
    i, j = pl.program_id(0), pl.program_id(1)
    x = x_ref[...]
    u = _dot(x, wu_ref[...])
    g_b = _dot(x, wb_ref[...])
    g_c = _dot(x, wc_ref[...])
    v = g_c * u
    bm = v.shape[0]
    (prev,) = _halo(carry_ref, (j, 0), [v[bm - SUBLANES:, :]], i % blocks_per_seq == 0)
    o_ref[...] = (g_b * _conv_body(v, cw_ref)).astype(o_ref.dtype)
    head = _conv_head(v[:SUBLANES, :], prev, cw_ref)
    o_ref[0:SUBLANES, :] = (g_b[:SUBLANES, :] * head).astype(o_ref.dtype)


def _shortconv(x, wu, wb, wc, conv_w, seq, *, bm, bn):
    m, k = x.shape
    n = wu.shape[1]
    bm, bn = _block(seq, bm), _block(n, bn)
    w_spec = pl.BlockSpec((k, bn), lambda i, j: (0, j))
    return pl.pallas_call(
        functools.partial(_shortconv_kernel, blocks_per_seq=seq // bm),
        grid=(m // bm, n // bn),
        in_specs=[
            pl.BlockSpec((bm, k), lambda i, j: (i, 0)),
            w_spec, w_spec, w_spec,
            pl.BlockSpec((CONV_TAPS, bn), lambda i, j: (0, j)),
        ],
        out_specs=pl.BlockSpec((bm, bn), lambda i, j: (i, j)),
        out_shape=jax.ShapeDtypeStruct((m, n), BF16),
        scratch_shapes=[pltpu.VMEM((n // bn, 1, SUBLANES, bn), F32)],
        compiler_params=_params(2),
        name="shortconv_proj",
    )(x, wu, wb, wc, conv_w)


def _silu(x):
    return x * jax.nn.sigmoid(x)


def _ffn_up_kernel(h_ref, wg_ref, wv_ref, cg_ref, cv_ref, o_ref, carry_ref, *, blocks_per_seq):
    i, j = pl.program_id(0), pl.program_id(1)
    h = h_ref[...]
    z_g = _dot(h, wg_ref[...])
    z_v = _dot(h, wv_ref[...])
    bm = z_g.shape[0]
    prev_g, prev_v = _halo(carry_ref, (j, 0, 1),
                           [z_g[bm - SUBLANES:, :], z_v[bm - SUBLANES:, :]],
                           i % blocks_per_seq == 0)
    o_ref[...] = (_silu(_conv_body(z_g, cg_ref)) * _conv_body(z_v, cv_ref)).astype(o_ref.dtype)
    head_g = _conv_head(z_g[:SUBLANES, :], prev_g, cg_ref)
    head_v = _conv_head(z_v[:SUBLANES, :], prev_v, cv_ref)
    o_ref[0:SUBLANES, :] = (_silu(head_g) * head_v).astype(o_ref.dtype)


def _ffn_up(h, wg, wv, cg, cv, seq, *, bm, bn):
    m, k = h.shape
    n = wg.shape[1]
    bm, bn = _block(seq, bm), _block(n, bn)
    w_spec = pl.BlockSpec((k, bn), lambda i, j: (0, j))
    c_spec = pl.BlockSpec((CONV_TAPS, bn), lambda i, j: (0, j))
    return pl.pallas_call(
        functools.partial(_ffn_up_kernel, blocks_per_seq=seq // bm),
        grid=(m // bm, n // bn),
        in_specs=[pl.BlockSpec((bm, k), lambda i, j: (i, 0)), w_spec, w_spec, c_spec, c_spec],
        out_specs=pl.BlockSpec((bm, bn), lambda i, j: (i, j)),
        out_shape=jax.ShapeDtypeStruct((m, n), BF16),
        scratch_shapes=[pltpu.VMEM((n // bn, 2, SUBLANES, bn), F32)],
        compiler_params=_params(2),
        name="ffn_up_conv_act",
    )(h, wg, wv, cg, cv)


def _attn_kernel(lq1_ref, lk1_ref, lq2_ref, lk2_ref, g_ref, q_ref, k_ref, v_ref, o_ref,
                 m_ref, l_ref, acc_ref, *, hd, tk, lam_init):
    qi = pl.program_id(2)
    tq = q_ref.shape[0]
    q = q_ref[...]
    qs = (q[:, :hd], q[:, hd:])

    m_ref[...] = jnp.full(m_ref.shape, MASK_VALUE, F32)
    l_ref[...] = jnp.zeros(l_ref.shape, F32)
    acc_ref[...] = jnp.zeros(acc_ref.shape, F32)

    def step(kb, mask):
        ks = pl.multiple_of(kb * tk, tk)
        k = k_ref[pl.ds(ks, tk), :]
        v = v_ref[pl.ds(ks, tk), :]
        for c in range(2):
            s = lax.dot_general(qs[c], k[:, c * hd:(c + 1) * hd], (((1,), (1,)), ((), ())),
                                preferred_element_type=F32)
            if mask is not None:
                s = jnp.where(mask, s, MASK_VALUE)
            m_old = m_ref[c]
            m_new = jnp.maximum(m_old, jnp.max(s, axis=-1, keepdims=True))
            alpha = jnp.exp2(m_old - m_new)
            p = jnp.exp2(s - m_new)
            l_ref[c] = alpha * l_ref[c] + jnp.sum(p, axis=-1, keepdims=True)
            acc_ref[c] = alpha * acc_ref[c] + _dot(p.astype(BF16), v)
            m_ref[c] = m_new

    def body(kb, carry):
        step(kb, None)
        return carry

    lax.fori_loop(0, qi * (tq // tk), body, 0)
    for d in range(tq // tk):
        row = lax.broadcasted_iota(jnp.int32, (tq, tk), 0)
        col = lax.broadcasted_iota(jnp.int32, (tq, tk), 1) + d * tk
        step(qi * (tq // tk) + d, col <= row)

    lam = (jnp.exp(jnp.sum(lq1_ref[...] * lk1_ref[...], axis=-1, keepdims=True))
           - jnp.exp(jnp.sum(lq2_ref[...] * lk2_ref[...], axis=-1, keepdims=True))
           + lam_init)
    o = acc_ref[0] / l_ref[0] - lam * (acc_ref[1] / l_ref[1])
    ms = jnp.mean(o * o, axis=-1, keepdims=True)
    o = o * lax.rsqrt(ms + RMS_EPS) * g_ref[...] * (1.0 - lam_init)
    o_ref[...] = o.astype(o_ref.dtype)


def _diff_attention(qkv, lq1, lk1, lq2, lk2, subln_g, *, batch, seq, heads, hd, lam_init, tq, tk):
    tq = _block(seq, tq)
    tk = _block(tq, tk)
    nq = seq // tq
    width = 2 * hd
    vec = pl.BlockSpec((1, hd), lambda b, h, i: (0, 0))
    return pl.pallas_call(
        functools.partial(_attn_kernel, hd=hd, tk=tk, lam_init=lam_init),
        grid=(batch, heads, nq),
        in_specs=[
            vec, vec, vec, vec,
            pl.BlockSpec((1, width), lambda b, h, i: (0, 0)),
            pl.BlockSpec((tq, width), lambda b, h, i: (b * nq + i, h)),
            pl.BlockSpec((seq, width), lambda b, h, i: (b, heads + h)),
            pl.BlockSpec((seq, width), lambda b, h, i: (b, 2 * heads + h)),
        ],
        out_specs=pl.BlockSpec((tq, width), lambda b, h, i: (b * nq + i, h)),
        out_shape=jax.ShapeDtypeStruct((batch * seq, heads * width), BF16),
        scratch_shapes=[
            pltpu.VMEM((2, tq, 1), F32),
            pltpu.VMEM((2, tq, 1), F32),
            pltpu.VMEM((2, tq, width), F32),
        ],
        compiler_params=_params(3),
        name="diff_attention",
    )(lq1, lk1, lq2, lk2, subln_g, qkv, qkv, qkv)


def _merge_kernel(x_ref, a_ref, c_ref, wga_ref, wgc_ref, wa_ref, wc_ref, o_ref):
    x = x_ref[...]
    gate_a = jax.nn.sigmoid(_dot(x, wga_ref[...]))
    gate_c = jax.nn.sigmoid(_dot(x, wgc_ref[...]))
    y_a = _dot(a_ref[...], wa_ref[...])
    y_c = _dot(c_ref[...], wc_ref[...])
    o_ref[...] = (gate_a * y_a + gate_c * y_c).astype(o_ref.dtype)


def _merge(x, attn, yc, wga, wgc, wa, wc, *, bm, bn):
    m, k = x.shape
    ka, kc = attn.shape[1], yc.shape[1]
    n = wga.shape[1]
    bm, bn = _block(m, bm), _block(n, bn)
    return pl.pallas_call(
        _merge_kernel,
        grid=(m // bm, n // bn),
        in_specs=[
            pl.BlockSpec((bm, k), lambda i, j: (i, 0)),
            pl.BlockSpec((bm, ka), lambda i, j: (i, 0)),
            pl.BlockSpec((bm, kc), lambda i, j: (i, 0)),
            pl.BlockSpec((k, bn), lambda i, j: (0, j)),
            pl.BlockSpec((k, bn), lambda i, j: (0, j)),
            pl.BlockSpec((ka, bn), lambda i, j: (0, j)),
            pl.BlockSpec((kc, bn), lambda i, j: (0, j)),
        ],
        out_specs=pl.BlockSpec((bm, bn), lambda i, j: (i, j)),
        out_shape=jax.ShapeDtypeStruct((m, n), BF16),
        compiler_params=_params(2),
        name="gated_merge",
    )(x, attn, yc, wga, wgc, wa, wc)


def _layer_norm_kernel(r_ref, g_ref, b_ref, *o_refs):
    r = r_ref[...]
    mu = jnp.mean(r, axis=-1, keepdims=True)
    rc = r - mu
    var = jnp.mean(rc * rc, axis=-1, keepdims=True)
    y = rc * lax.rsqrt(var + LN_EPS) * g_ref[...] + b_ref[...]
    for o_ref in o_refs:
        o_ref[...] = y.astype(o_ref.dtype)


def _layer_norm(r, g, b, out_dtypes, *, bm, name):
    m, d = r.shape
    bm = _block(m, bm)
    row = pl.BlockSpec((bm, d), lambda i: (i, 0))
    vec = pl.BlockSpec((1, d), lambda i: (0, 0))
    return pl.pallas_call(
        _layer_norm_kernel,
        grid=(m // bm,),
        in_specs=[row, vec, vec],
        out_specs=[row] * len(out_dtypes),
        out_shape=[jax.ShapeDtypeStruct((m, d), dt) for dt in out_dtypes],
        compiler_params=_params(1),
        name=name,
    )(r, g, b)


def _lambda_init(layer_idx):
    return 0.8 - 0.6 * math.exp(-0.3 * layer_idx)


def kernel(x, w_in, lambda_q1, lambda_k1, lambda_q2, lambda_k2, subln_g, conv_mix_w, w_attn_out,
           w_conv_out, w_o, ln1_g, ln1_b, w_up, ffn_conv_w, w_down, ln2_g, ln2_b):
    batch, seq, d_model = x.shape
    depth = w_in.shape[0]
    hd = lambda_q1.shape[-1]
    v_width = w_attn_out.shape[1]
    heads = v_width // (2 * hd)
    qk_width = heads * 2 * hd
    conv_width = conv_mix_w.shape[-1]
    d_ff = w_down.shape[1]
    alpha = (2.0 * depth) ** 0.25
    rows = batch * seq

    o_k = qk_width
    o_v = o_k + qk_width
    o_u = o_v + v_width
    o_b = o_u + conv_width
    o_c = o_b + conv_width
    o_ga = o_c + conv_width
    o_gc = o_ga + d_model

    h = x.reshape(rows, d_model)
    for l in range(depth):
        lam_init = _lambda_init(l)
        h_bf = h.astype(BF16)
        w_l = w_in[l]
        w_qkv = w_l[:, :o_u].astype(BF16)
        w_u = w_l[:, o_u:o_b].astype(BF16)
        w_b = w_l[:, o_b:o_c].astype(BF16)
        w_c = w_l[:, o_c:o_ga].astype(BF16)
        w_ga = w_l[:, o_ga:o_gc].astype(BF16)
        w_gc = w_l[:, o_gc:].astype(BF16)

        q_scale = LOG2E / math.sqrt(hd)
        col_scale = jnp.concatenate(
            [jnp.full((1, qk_width), q_scale, F32), jnp.ones((1, o_u - qk_width), F32)], axis=1)
        qkv = _proj_scale(h_bf, w_qkv, col_scale, bm=1024, bn=1024, name="qkv_proj")

        attn = _diff_attention(
            qkv, lambda_q1[l][None], lambda_k1[l][None], lambda_q2[l][None], lambda_k2[l][None],
            subln_g[l][None], batch=batch, seq=seq, heads=heads, hd=hd, lam_init=lam_init,
            tq=512, tk=512)

        yc = _shortconv(h_bf, w_u, w_b, w_c, conv_mix_w[l], seq, bm=1024, bn=256)

        merged = _merge(h_bf, attn, yc, w_ga, w_gc, w_attn_out[l].astype(BF16),
                        w_conv_out[l].astype(BF16), bm=512, bn=256)

        r1 = _proj_residual(merged, w_o[l].astype(BF16), h, alpha, bm=1024, bn=512, name="wo_proj")
        h1, h1_bf = _layer_norm(r1, ln1_g[l][None], ln1_b[l][None], (F32, BF16), bm=256, name="ln1")

        w_up_l = w_up[l]
        cw = ffn_conv_w[l]
        act = _ffn_up(h1_bf, w_up_l[:, :d_ff].astype(BF16), w_up_l[:, d_ff:].astype(BF16),
                      cw[:, :d_ff], cw[:, d_ff:], seq, bm=1024, bn=256)

        r2 = _proj_residual(act, w_down[l].astype(BF16), h1, alpha, bm=512, bn=512, name="down_proj")
        (h,) = _layer_norm(r2, ln2_g[l][None], ln2_b[l][None], (F32,), bm=256, name="ln2")
    return h.reshape(batch, seq, d_model)
```

```python
import functools
import math

import jax
import jax.numpy as jnp
from jax import lax
from jax.experimental import pallas as pl
from jax.experimental.pallas import tpu as pltpu

LN_EPS = 1e-5
RMS_EPS = 1e-5
LOG2E = 1.4426950408889634
CONV_TAPS = 3
SUBLANES = 8
LANES = 128
MASK_VALUE = -1e30
VMEM_LIMIT_BYTES = 56 * 1024 * 1024

F32 = jnp.float32
BF16 = jnp.bfloat16


def _params(n_axes):
    return pltpu.CompilerParams(
        dimension_semantics=("arbitrary",) * n_axes,
        vmem_limit_bytes=VMEM_LIMIT_BYTES,
    )


def _block(dim, want):
    if dim <= want:
        return dim
    b = want
    while dim % b:
        b //= 2
    return b


def _dot(a, b):
    return jnp.dot(a, b, preferred_element_type=F32)


def _proj_scale_kernel(x_ref, w_ref, s_ref, o_ref):
    o_ref[...] = (_dot(x_ref[...], w_ref[...]) * s_ref[...]).astype(o_ref.dtype)


def _proj_scale(x, w, col_scale, *, bm, bn, name):
    m, k = x.shape
    n = w.shape[1]
    bm, bn = _block(m, bm), _block(n, bn)
    return pl.pallas_call(
        _proj_scale_kernel,
        grid=(m // bm, n // bn),
        in_specs=[
            pl.BlockSpec((bm, k), lambda i, j: (i, 0)),
            pl.BlockSpec((k, bn), lambda i, j: (0, j)),
            pl.BlockSpec((1, bn), lambda i, j: (0, j)),
        ],
        out_specs=pl.BlockSpec((bm, bn), lambda i, j: (i, j)),
        out_shape=jax.ShapeDtypeStruct((m, n), BF16),
        compiler_params=_params(2),
        name=name,
    )(x, w, col_scale)


def _proj_residual_kernel(x_ref, w_ref, r_ref, o_ref, *, alpha):
    o_ref[...] = alpha * r_ref[...] + _dot(x_ref[...], w_ref[...])


def _proj_residual(x, w, resid, alpha, *, bm, bn, name):
    m, k = x.shape
    n = w.shape[1]
    bm, bn = _block(m, bm), _block(n, bn)
    return pl.pallas_call(
        functools.partial(_proj_residual_kernel, alpha=alpha),
        grid=(m // bm, n // bn),
        in_specs=[
            pl.BlockSpec((bm, k), lambda i, j: (i, 0)),
            pl.BlockSpec((k, bn), lambda i, j: (0, j)),
            pl.BlockSpec((bm, bn), lambda i, j: (i, j)),
        ],
        out_specs=pl.BlockSpec((bm, bn), lambda i, j: (i, j)),
        out_shape=jax.ShapeDtypeStruct((m, n), F32),
        compiler_params=_params(2),
        name=name,
    )(x, w, resid)


def _conv_body(z, w_ref):
    z1 = pltpu.roll(z, 1, 0)
    z2 = pltpu.roll(z, 2, 0)
    return w_ref[0:1, :] * z2 + w_ref[1:2, :] * z1 + w_ref[2:3, :] * z


def _conv_head(z_head, prev, w_ref):
    row = lax.broadcasted_iota(jnp.int32, z_head.shape, 0)
    z1 = jnp.where(row == 0, prev[SUBLANES - 1:SUBLANES, :], pltpu.roll(z_head, 1, 0))
    z2 = jnp.where(row == 0, prev[SUBLANES - 2:SUBLANES - 1, :],
                   jnp.where(row == 1, prev[SUBLANES - 1:SUBLANES, :], pltpu.roll(z_head, 2, 0)))
    return w_ref[0:1, :] * z2 + w_ref[1:2, :] * z1 + w_ref[2:3, :] * z_head


def _halo(carry_ref, idx, new_tail, seq_start):
    @pl.when(seq_start)
    def _():
        for i in idx[1:]:
            carry_ref[idx[0], i] = jnp.zeros(carry_ref.shape[2:], F32)

    prev = [carry_ref[idx[0], i] for i in idx[1:]]
    for i, t in zip(idx[1:], new_tail):
        carry_ref[idx[0], i] = t
    return prev


def _shortconv_kernel(x_ref, wu_ref, wb_ref, wc_ref, cw_ref, o_ref, carry_ref, *, blocks_per_seq):
    i, j = pl.program_id(0), pl.program_id(1)
    x = x_ref[...]
    u = _dot(x, wu_ref[...])
    g_b = _dot(x, wb_ref[...])
    g_c = _dot(x, wc_ref[...])
    v = g_c * u
    bm = v.shape[0]
    (prev,) = _halo(carry_ref, (j, 0), [v[bm - SUBLANES:, :]], i % blocks_per_seq == 0)
    o_ref[...] = (g_b * _conv_body(v, cw_ref)).astype(o_ref.dtype)
    head = _conv_head(v[:SUBLANES, :], prev, cw_ref)
    o_ref[0:SUBLANES, :] = (g_b[:SUBLANES, :] * head).astype(o_ref.dtype)


def _shortconv(x, wu, wb, wc, conv_w, seq, *, bm, bn):
    m, k = x.shape
    n = wu.shape[1]
    bm, bn = _block(seq, bm), _block(n, bn)
    w_spec = pl.BlockSpec((k, bn), lambda i, j: (0, j))
    return pl.pallas_call(
        functools.partial(_shortconv_kernel, blocks_per_seq=seq // bm),
        grid=(m // bm, n // bn),
        in_specs=[
            pl.BlockSpec((bm, k), lambda i, j: (i, 0)),
            w_spec, w_spec, w_spec,
            pl.BlockSpec((CONV_TAPS, bn), lambda i, j: (0, j)),
        ],
        out_specs=pl.BlockSpec((bm, bn), lambda i, j: (i, j)),
        out_shape=jax.ShapeDtypeStruct((m, n), BF16),
        scratch_shapes=[pltpu.VMEM((n // bn, 1, SUBLANES, bn), F32)],
        compiler_params=_params(2),
        name="shortconv_proj",
    )(x, wu, wb, wc, conv_w)


def _silu(x):
    return x * jax.nn.sigmoid(x)


def _ffn_up_kernel(h_ref, wg_ref, wv_ref, cg_ref, cv_ref, o_ref, carry_ref, *, blocks_per_seq):
    i, j = pl.program_id(0), pl.program_id(1)
    h = h_ref[...]
    z_g = _dot(h, wg_ref[...])
    z_v = _dot(h, wv_ref[...])
    bm = z_g.shape[0]
    prev_g, prev_v = _halo(carry_ref, (j, 0, 1),
                           [z_g[bm - SUBLANES:, :], z_v[bm - SUBLANES:, :]],
                           i % blocks_per_seq == 0)
    o_ref[...] = (_silu(_conv_body(z_g, cg_ref)) * _conv_body(z_v, cv_ref)).astype(o_ref.dtype)
    head_g = _conv_head(z_g[:SUBLANES, :], prev_g, cg_ref)
    head_v = _conv_head(z_v[:SUBLANES, :], prev_v, cv_ref)
    o_ref[0:SUBLANES, :] = (_silu(head_g) * head_v).astype(o_ref.dtype)


def _ffn_up(h, wg, wv, cg, cv, seq, *, bm, bn):
    m, k = h.shape
    n = wg.shape[1]
    bm, bn = _block(seq, bm), _block(n, bn)
    w_spec = pl.BlockSpec((k, bn), lambda i, j: (0, j))
    c_spec = pl.BlockSpec((CONV_TAPS, bn), lambda i, j: (0, j))
    return pl.pallas_call(
        functools.partial(_ffn_up_kernel, blocks_per_seq=seq // bm),
        grid=(m // bm, n // bn),
        in_specs=[pl.BlockSpec((bm, k), lambda i, j: (i, 0)), w_spec, w_spec, c_spec, c_spec],
        out_specs=pl.BlockSpec((bm, bn), lambda i, j: (i, j)),
        out_shape=jax.ShapeDtypeStruct((m, n), BF16),
        scratch_shapes=[pltpu.VMEM((n // bn, 2, SUBLANES, bn), F32)],
        compiler_params=_params(2),
        name="ffn_up_conv_act",
    )(h, wg, wv, cg, cv)


def _attn_kernel(lq1_ref, lk1_ref, lq2_ref, lk2_ref, g_ref, q_ref, k_ref, v_ref, o_ref,
                 m_ref, l_ref, alpha_ref, acc_ref, s_ref, p_ref, *, hd, lam_init):
    qi = pl.program_id(2)
    t = q_ref.shape[0]
    n_lane_tiles = t // LANES

    m_ref[...] = jnp.full(m_ref.shape, MASK_VALUE, F32)
    l_ref[...] = jnp.zeros(l_ref.shape, F32)
    acc_ref[...] = jnp.zeros(acc_ref.shape, F32)

    def scores(kb, slot):
        k = k_ref[pl.ds(pl.multiple_of(kb * t, t), t), :]
        for c in range(2):
            s_ref[slot, c] = lax.dot_general(
                q_ref[:, c * hd:(c + 1) * hd], k[:, c * hd:(c + 1) * hd], (((1,), (1,)), ((), ())),
                preferred_element_type=F32)

    def softmax_rows(slot, c, r, diagonal):
        rows = pl.ds(r, LANES)
        jd = r // LANES
        n_tiles = jd + 1 if diagonal else n_lane_tiles
        tiles = [s_ref[slot, c, rows, j * LANES:(j + 1) * LANES] for j in range(n_tiles)]
        if diagonal:
            tri = (lax.broadcasted_iota(jnp.int32, (LANES, LANES), 1)
                   <= lax.broadcasted_iota(jnp.int32, (LANES, LANES), 0))
            tiles[jd] = jnp.where(tri, tiles[jd], MASK_VALUE)
        m_old = m_ref[c, rows, :]
        m_new = jnp.maximum(m_old, jnp.max(functools.reduce(jnp.maximum, tiles), axis=-1, keepdims=True))
        alpha = jnp.exp2(m_old - m_new)
        p_tiles = [jnp.exp2(tile - m_new) for tile in tiles]
        l_ref[c, rows, :] = alpha * l_ref[c, rows, :] + functools.reduce(jnp.add, p_tiles)
        p_tiles += [jnp.zeros((LANES, LANES), F32)] * (n_lane_tiles - n_tiles)
        p_ref[c, rows, :] = jnp.concatenate(p_tiles, axis=1).astype(BF16)
        alpha_ref[c, rows, :] = alpha
        m_ref[c, rows, :] = m_new

    def accumulate(kb, slot, diagonal):
        v = v_ref[pl.ds(pl.multiple_of(kb * t, t), t), :]
        for c in range(2):
            for r in range(0, t, LANES):
                softmax_rows(slot, c, r, diagonal)
            alpha = jnp.concatenate([alpha_ref[c]] * (acc_ref.shape[-1] // LANES), axis=1)
            acc_ref[c] = alpha * acc_ref[c] + _dot(p_ref[c], v)

    def pair(i, carry):
        kb = 2 * i
        scores(kb + 1, 1)
        accumulate(kb, 0, False)
        scores(kb + 2, 0)
        accumulate(kb + 1, 1, False)
        return carry

    scores(0, 0)
    lax.fori_loop(0, qi // 2, pair, 0)

    @pl.when(qi % 2 == 0)
    def _():
        accumulate(qi, 0, True)

    @pl.when(qi % 2 == 1)
    def _():
        scores(qi, 1)
        accumulate(qi - 1, 0, False)
        accumulate(qi, 1, True)

    lam = (jnp.exp(jnp.sum(lq1_ref[...] * lk1_ref[...], axis=-1, keepdims=True))
           - jnp.exp(jnp.sum(lq2_ref[...] * lk2_ref[...], axis=-1, keepdims=True))
           + lam_init)
    l1 = jnp.sum(l_ref[0], axis=-1, keepdims=True)
    l2 = jnp.sum(l_ref[1], axis=-1, keepdims=True)
    o = acc_ref[0] / l1 - lam * (acc_ref[1] / l2)
    ms = jnp.mean(o * o, axis=-1, keepdims=True)
    o = o * lax.rsqrt(ms + RMS_EPS) * g_ref[...] * (1.0 - lam_init)
    o_ref[...] = o.astype(o_ref.dtype)


def _diff_attention(qkv, lq1, lk1, lq2, lk2, subln_g, *, batch, seq, heads, hd, lam_init, tq):
    tq = _block(seq, tq)
    assert tq % LANES == 0, "query blocks are processed in LANES-row chunks"
    nq = seq // tq
    width = 2 * hd
    vec = pl.BlockSpec((1, hd), lambda b, h, i: (0, 0))
    return pl.pallas_call(
        functools.partial(_attn_kernel, hd=hd, lam_init=lam_init),
        grid=(batch, heads, nq),
        in_specs=[
            vec, vec, vec, vec,
            pl.BlockSpec((1, width), lambda b, h, i: (0, 0)),
            pl.BlockSpec((tq, width), lambda b, h, i: (b * nq + i, h)),
            pl.BlockSpec((seq, width), lambda b, h, i: (b, heads + h)),
            pl.BlockSpec((seq, width), lambda b, h, i: (b, 2 * heads + h)),
        ],
        out_specs=pl.BlockSpec((tq, width), lambda b, h, i: (b * nq + i, h)),
        out_shape=jax.ShapeDtypeStruct((batch * seq, heads * width), BF16),
        scratch_shapes=[
            pltpu.VMEM((2, tq, LANES), F32),
            pltpu.VMEM((2, tq, LANES), F32),
            pltpu.VMEM((2, tq, LANES), F32),
            pltpu.VMEM((2, tq, width), F32),
            pltpu.VMEM((2, 2, tq, tq), F32),
            pltpu.VMEM((2, tq, tq), BF16),
        ],
        compiler_params=_params(3),
        name="diff_attention",
    )(lq1, lk1, lq2, lk2, subln_g, qkv, qkv, qkv)


def _merge_kernel(x_ref, a_ref, c_ref, wga_ref, wgc_ref, wa_ref, wc_ref, o_ref):
    x = x_ref[...]
    gate_a = jax.nn.sigmoid(_dot(x, wga_ref[...]))
    gate_c = jax.nn.sigmoid(_dot(x, wgc_ref[...]))
    y_a = _dot(a_ref[...], wa_ref[...])
    y_c = _dot(c_ref[...], wc_ref[...])
    o_ref[...] = (gate_a * y_a + gate_c * y_c).astype(o_ref.dtype)


def _merge(x, attn, yc, wga, wgc, wa, wc, *, bm, bn):
    m, k = x.shape
    ka, kc = attn.shape[1], yc.shape[1]
    n = wga.shape[1]
    bm, bn = _block(m, bm), _block(n, bn)
    return pl.pallas_call(
        _merge_kernel,
        grid=(m // bm, n // bn),
        in_specs=[
            pl.BlockSpec((bm, k), lambda i, j: (i, 0)),
            pl.BlockSpec((bm, ka), lambda i, j: (i, 0)),
            pl.BlockSpec((bm, kc), lambda i, j: (i, 0)),
            pl.BlockSpec((k, bn), lambda i, j: (0, j)),
            pl.BlockSpec((k, bn), lambda i, j: (0, j)),
            pl.BlockSpec((ka, bn), lambda i, j: (0, j)),
            pl.BlockSpec((kc, bn), lambda i, j: (0, j)),
        ],
        out_specs=pl.BlockSpec((bm, bn), lambda i, j: (i, j)),
        out_shape=jax.ShapeDtypeStruct((m, n), BF16),
        compiler_params=_params(2),
        name="gated_merge",
    )(x, attn, yc, wga, wgc, wa, wc)


def _layer_norm_kernel(r_ref, g_ref, b_ref, *o_refs):
    r = r_ref[...]
    mu = jnp.mean(r, axis=-1, keepdims=True)
    rc = r - mu
    var = jnp.mean(rc * rc, axis=-1, keepdims=True)
    y = rc * lax.rsqrt(var + LN_EPS) * g_ref[...] + b_ref[...]
    for o_ref in o_refs:
        o_ref[...] = y.astype(o_ref.dtype)


def _layer_norm(r, g, b, out_dtypes, *, bm, name):
    m, d = r.shape
    bm = _block(m, bm)
    row = pl.BlockSpec((bm, d), lambda i: (i, 0))
    vec = pl.BlockSpec((1, d), lambda i: (0, 0))
    return pl.pallas_call(
        _layer_norm_kernel,
        grid=(m // bm,),
        in_specs=[row, vec, vec],
        out_specs=[row] * len(out_dtypes),
        out_shape=[jax.ShapeDtypeStruct((m, d), dt) for dt in out_dtypes],
        compiler_params=_params(1),
        name=name,
    )(r, g, b)


def _lambda_init(layer_idx):
    return 0.8 - 0.6 * math.exp(-0.3 * layer_idx)


def kernel(x, w_in, lambda_q1, lambda_k1, lambda_q2, lambda_k2, subln_g, conv_mix_w, w_attn_out,
           w_conv_out, w_o, ln1_g, ln1_b, w_up, ffn_conv_w, w_down, ln2_g, ln2_b):
    batch, seq, d_model = x.shape
    depth = w_in.shape[0]
    hd = lambda_q1.shape[-1]
    v_width = w_attn_out.shape[1]
    heads = v_width // (2 * hd)
    qk_width = heads * 2 * hd
    conv_width = conv_mix_w.shape[-1]
    d_ff = w_down.shape[1]
    alpha = (2.0 * depth) ** 0.25
    rows = batch * seq

    o_k = qk_width
    o_v = o_k + qk_width
    o_u = o_v + v_width
    o_b = o_u + conv_width
    o_c = o_b + conv_width
    o_ga = o_c + conv_width
    o_gc = o_ga + d_model

    h = x.reshape(rows, d_model)
    for l in range(depth):
        lam_init = _lambda_init(l)
        h_bf = h.astype(BF16)
        w_l = w_in[l]
        w_qkv = w_l[:, :o_u].astype(BF16)
        w_u = w_l[:, o_u:o_b].astype(BF16)
        w_b = w_l[:, o_b:o_c].astype(BF16)
        w_c = w_l[:, o_c:o_ga].astype(BF16)
        w_ga = w_l[:, o_ga:o_gc].astype(BF16)
        w_gc = w_l[:, o_gc:].astype(BF16)

        q_scale = LOG2E / math.sqrt(hd)
        col_scale = jnp.concatenate(
            [jnp.full((1, qk_width), q_scale, F32), jnp.ones((1, o_u - qk_width), F32)], axis=1)
        qkv = _proj_scale(h_bf, w_qkv, col_scale, bm=1024, bn=1024, name="qkv_proj")

        attn = _diff_attention(
            qkv, lambda_q1[l][None], lambda_k1[l][None], lambda_q2[l][None], lambda_k2[l][None],
            subln_g[l][None], batch=batch, seq=seq, heads=heads, hd=hd, lam_init=lam_init,
            tq=512)

        yc = _shortconv(h_bf, w_u, w_b, w_c, conv_mix_w[l], seq, bm=1024, bn=256)

        merged = _merge(h_bf, attn, yc, w_ga, w_gc, w_attn_out[l].astype(BF16),
                        w_conv_out[l].astype(BF16), bm=512, bn=256)

        r1 = _proj_residual(merged, w_o[l].astype(BF16), h, alpha, bm=1024, bn=512, name="wo_proj")
        h1, h1_bf = _layer_norm(r1, ln1_g[l][None], ln1_b[l][None], (F32, BF16), bm=256, name="ln1")

        w_up_l = w_up[l]
        cw = ffn_conv_w[l]
        act = _ffn_up(h1_bf, w_up_l[:, :d_ff].astype(BF16), w_up_l[:, d_ff:].astype(BF16),
                      cw[:, :d_ff], cw[:, d_ff:], seq, bm=1024, bn=256)

        r2 = _proj_residual(act, w_down[l].astype(BF16), h1, alpha, bm=512, bn=512, name="down_proj")
        (h,) = _layer_norm(r2, ln2_g[l][None], ln2_b[l][None], (F32,), bm=256, name="ln2")
    return h.reshape(batch, seq, d_model)
```

```python
import functools
import math

import jax
import jax.numpy as jnp
from jax import lax
from jax.experimental import pallas as pl
from jax.experimental.pallas import tpu as pltpu

LN_EPS = 1e-5
RMS_EPS = 1e-5
LOG2E = 1.4426950408889634
CONV_TAPS = 3
SUBLANES = 8
LANES = 128
MASK_VALUE = -1e30
VMEM_LIMIT_BYTES = 56 * 1024 * 1024

TILES = {
    "qkv_proj": (1024, 1024),
    "shortconv_proj": (1024, 256),
    "gated_merge": (1024, 256),
    "wo_proj": (1024, 512),
    "ffn_up_conv_act": (2048, 256),
    "down_proj": (512, 512),
}
ATTN_BLOCK = 512
LN_ROWS = 256

F32 = jnp.float32
BF16 = jnp.bfloat16


def _params(n_axes):
    return pltpu.CompilerParams(
        dimension_semantics=("arbitrary",) * n_axes,
        vmem_limit_bytes=VMEM_LIMIT_BYTES,
    )


def _block(dim, want):
    if dim <= want:
        return dim
    b = want
    while dim % b:
        b //= 2
    return b


def _dot(a, b):
    return jnp.dot(a, b, preferred_element_type=F32)


def _row_spec(bm, k):
    return pl.BlockSpec((bm, k), lambda i, j: (i, 0), pipeline_mode=pl.Buffered(1))


def _col_spec(rows, bn, offset):
    assert offset % bn == 0, (offset, bn)
    return pl.BlockSpec((rows, bn), lambda i, j: (0, offset // bn + j))


def _proj_scale_kernel(x_ref, w_ref, s_ref, o_ref):
    o_ref[...] = (_dot(x_ref[...], w_ref[...]) * s_ref[...]).astype(o_ref.dtype)


def _proj_scale(x, w, col_scale, name):
    m, k = x.shape
    n = col_scale.shape[1]
    bm, bn = _block(m, TILES[name][0]), _block(n, TILES[name][1])
    return pl.pallas_call(
        _proj_scale_kernel,
        grid=(m // bm, n // bn),
        in_specs=[_row_spec(bm, k), _col_spec(k, bn, 0), _col_spec(1, bn, 0)],
        out_specs=pl.BlockSpec((bm, bn), lambda i, j: (i, j)),
        out_shape=jax.ShapeDtypeStruct((m, n), BF16),
        compiler_params=_params(2),
        name=name,
    )(x, w, col_scale)


def _proj_residual_kernel(x_ref, w_ref, r_ref, o_ref, *, alpha):
    o_ref[...] = alpha * r_ref[...] + _dot(x_ref[...], w_ref[...])


def _proj_residual(x, w, resid, alpha, name):
    m, k = x.shape
    n = w.shape[1]
    bm, bn = _block(m, TILES[name][0]), _block(n, TILES[name][1])
    return pl.pallas_call(
        functools.partial(_proj_residual_kernel, alpha=alpha),
        grid=(m // bm, n // bn),
        in_specs=[_row_spec(bm, k), _col_spec(k, bn, 0), pl.BlockSpec((bm, bn), lambda i, j: (i, j))],
        out_specs=pl.BlockSpec((bm, bn), lambda i, j: (i, j)),
        out_shape=jax.ShapeDtypeStruct((m, n), F32),
        compiler_params=_params(2),
        name=name,
    )(x, w, resid)


def _conv_body(z, w_ref):
    z1 = pltpu.roll(z, 1, 0)
    z2 = pltpu.roll(z, 2, 0)
    return w_ref[0:1, :] * z2 + w_ref[1:2, :] * z1 + w_ref[2:3, :] * z


def _conv_head(z_head, prev, w_ref):
    row = lax.broadcasted_iota(jnp.int32, z_head.shape, 0)
    z1 = jnp.where(row == 0, prev[SUBLANES - 1:SUBLANES, :], pltpu.roll(z_head, 1, 0))
    z2 = jnp.where(row == 0, prev[SUBLANES - 2:SUBLANES - 1, :],
                   jnp.where(row == 1, prev[SUBLANES - 1:SUBLANES, :], pltpu.roll(z_head, 2, 0)))
    return w_ref[0:1, :] * z2 + w_ref[1:2, :] * z1 + w_ref[2:3, :] * z_head


def _halo(carry_ref, idx, new_tail, seq_start):
    @pl.when(seq_start)
    def _():
        for i in idx[1:]:
            carry_ref[idx[0], i] = jnp.zeros(carry_ref.shape[2:], F32)

    prev = [carry_ref[idx[0], i] for i in idx[1:]]
    for i, t in zip(idx[1:], new_tail):
        carry_ref[idx[0], i] = t
    return prev


def _shortconv_kernel(x_ref, wu_ref, wb_ref, wc_ref, cw_ref, o_ref, carry_ref, *, blocks_per_seq):
    i, j = pl.program_id(0), pl.program_id(1)
    x = x_ref[...]
    u = _dot(x, wu_ref[...])
    g_b = _dot(x, wb_ref[...])
    g_c = _dot(x, wc_ref[...])
    v = g_c * u
    bm = v.shape[0]
    (prev,) = _halo(carry_ref, (j, 0), [v[bm - SUBLANES:, :]], i % blocks_per_seq == 0)
    o_ref[...] = (g_b * _conv_body(v, cw_ref)).astype(o_ref.dtype)
    head = _conv_head(v[:SUBLANES, :], prev, cw_ref)
    o_ref[0:SUBLANES, :] = (g_b[:SUBLANES, :] * head).astype(o_ref.dtype)


def _shortconv(x, w, offsets, conv_w, seq):
    name = "shortconv_proj"
    m, k = x.shape
    n = conv_w.shape[1]
    bm, bn = _block(seq, TILES[name][0]), _block(n, TILES[name][1])
    return pl.pallas_call(
        functools.partial(_shortconv_kernel, blocks_per_seq=seq // bm),
        grid=(m // bm, n // bn),
        in_specs=[_row_spec(bm, k)] + [_col_spec(k, bn, o) for o in offsets] + [_col_spec(CONV_TAPS, bn, 0)],
        out_specs=pl.BlockSpec((bm, bn), lambda i, j: (i, j)),
        out_shape=jax.ShapeDtypeStruct((m, n), BF16),
        scratch_shapes=[pltpu.VMEM((n // bn, 1, SUBLANES, bn), F32)],
        compiler_params=_params(2),
        name=name,
    )(x, w, w, w, conv_w)


def _silu(x):
    return x * jax.nn.sigmoid(x)


def _ffn_up_kernel(h_ref, wg_ref, wv_ref, cg_ref, cv_ref, o_ref, carry_ref, *, blocks_per_seq):
    i, j = pl.program_id(0), pl.program_id(1)
    h = h_ref[...]
    z_g = _dot(h, wg_ref[...])
    z_v = _dot(h, wv_ref[...])
    bm = z_g.shape[0]
    prev_g, prev_v = _halo(carry_ref, (j, 0, 1),
                           [z_g[bm - SUBLANES:, :], z_v[bm - SUBLANES:, :]],
                           i % blocks_per_seq == 0)
    o_ref[...] = (_silu(_conv_body(z_g, cg_ref)) * _conv_body(z_v, cv_ref)).astype(o_ref.dtype)
    head_g = _conv_head(z_g[:SUBLANES, :], prev_g, cg_ref)
    head_v = _conv_head(z_v[:SUBLANES, :], prev_v, cv_ref)
    o_ref[0:SUBLANES, :] = (_silu(head_g) * head_v).astype(o_ref.dtype)


def _ffn_up(h, w_up, conv_w, seq):
    name = "ffn_up_conv_act"
    m, k = h.shape
    n = w_up.shape[1] // 2
    bm, bn = _block(seq, TILES[name][0]), _block(n, TILES[name][1])
    return pl.pallas_call(
        functools.partial(_ffn_up_kernel, blocks_per_seq=seq // bm),
        grid=(m // bm, n // bn),
        in_specs=[_row_spec(bm, k), _col_spec(k, bn, 0), _col_spec(k, bn, n),
                  _col_spec(CONV_TAPS, bn, 0), _col_spec(CONV_TAPS, bn, n)],
        out_specs=pl.BlockSpec((bm, bn), lambda i, j: (i, j)),
        out_shape=jax.ShapeDtypeStruct((m, n), BF16),
        scratch_shapes=[pltpu.VMEM((n // bn, 2, SUBLANES, bn), F32)],
        compiler_params=_params(2),
        name=name,
    )(h, w_up, w_up, conv_w, conv_w)


def _attn_kernel(lq1_ref, lk1_ref, lq2_ref, lk2_ref, g_ref, q_ref, k_ref, v_ref, o_ref,
                 m_ref, l_ref, alpha_ref, acc_ref, s_ref, p_ref, *, hd, lam_init):
    qi = pl.program_id(2)
    t = q_ref.shape[0]
    n_lane_tiles = t // LANES

    m_ref[...] = jnp.full(m_ref.shape, MASK_VALUE, F32)
    l_ref[...] = jnp.zeros(l_ref.shape, F32)
    acc_ref[...] = jnp.zeros(acc_ref.shape, F32)

    def scores(kb, slot):
        k = k_ref[pl.ds(pl.multiple_of(kb * t, t), t), :]
        for c in range(2):
            s_ref[slot, c] = lax.dot_general(
                q_ref[:, c * hd:(c + 1) * hd], k[:, c * hd:(c + 1) * hd], (((1,), (1,)), ((), ())),
                preferred_element_type=F32)

    def softmax_rows(slot, c, r, diagonal):
        rows = pl.ds(r, LANES)
        jd = r // LANES
        n_tiles = jd + 1 if diagonal else n_lane_tiles
        tiles = [s_ref[slot, c, rows, j * LANES:(j + 1) * LANES] for j in range(n_tiles)]
        if diagonal:
            tri = (lax.broadcasted_iota(jnp.int32, (LANES, LANES), 1)
                   <= lax.broadcasted_iota(jnp.int32, (LANES, LANES), 0))
            tiles[jd] = jnp.where(tri, tiles[jd], MASK_VALUE)
        m_old = m_ref[c, rows, :]
        m_new = jnp.maximum(m_old, jnp.max(functools.reduce(jnp.maximum, tiles), axis=-1, keepdims=True))
        alpha = jnp.exp2(m_old - m_new)
        p_tiles = [jnp.exp2(tile - m_new) for tile in tiles]
        l_ref[c, rows, :] = alpha * l_ref[c, rows, :] + functools.reduce(jnp.add, p_tiles)
        p_tiles += [jnp.zeros((LANES, LANES), F32)] * (n_lane_tiles - n_tiles)
        p_ref[c, rows, :] = jnp.concatenate(p_tiles, axis=1).astype(BF16)
        alpha_ref[c, rows, :] = alpha
        m_ref[c, rows, :] = m_new

    def accumulate(kb, slot, diagonal):
        v = v_ref[pl.ds(pl.multiple_of(kb * t, t), t), :]
        for c in range(2):
            for r in range(0, t, LANES):
                softmax_rows(slot, c, r, diagonal)
            alpha = jnp.concatenate([alpha_ref[c]] * (acc_ref.shape[-1] // LANES), axis=1)
            acc_ref[c] = alpha * acc_ref[c] + _dot(p_ref[c], v)

    def pair(i, carry):
        kb = 2 * i
        scores(kb + 1, 1)
        accumulate(kb, 0, False)
        scores(kb + 2, 0)
        accumulate(kb + 1, 1, False)
        return carry

    scores(0, 0)
    lax.fori_loop(0, qi // 2, pair, 0)

    @pl.when(qi % 2 == 0)
    def _():
        accumulate(qi, 0, True)

    @pl.when(qi % 2 == 1)
    def _():
        scores(qi, 1)
        accumulate(qi - 1, 0, False)
        accumulate(qi, 1, True)

    lam = (jnp.exp(jnp.sum(lq1_ref[...] * lk1_ref[...], axis=-1, keepdims=True))
           - jnp.exp(jnp.sum(lq2_ref[...] * lk2_ref[...], axis=-1, keepdims=True))
           + lam_init)
    l1 = jnp.sum(l_ref[0], axis=-1, keepdims=True)
    l2 = jnp.sum(l_ref[1], axis=-1, keepdims=True)
    o = acc_ref[0] / l1 - lam * (acc_ref[1] / l2)
    ms = jnp.mean(o * o, axis=-1, keepdims=True)
    o = o * lax.rsqrt(ms + RMS_EPS) * g_ref[...] * (1.0 - lam_init)
    o_ref[...] = o.astype(o_ref.dtype)


def _diff_attention(qkv, lq1, lk1, lq2, lk2, subln_g, *, batch, seq, heads, hd, lam_init):
    tq = _block(seq, ATTN_BLOCK)
    assert tq % LANES == 0, "query blocks are processed in LANES-row chunks"
    nq = seq // tq
    width = 2 * hd
    vec = pl.BlockSpec((1, hd), lambda b, h, i: (0, 0))
    return pl.pallas_call(
        functools.partial(_attn_kernel, hd=hd, lam_init=lam_init),
        grid=(batch, heads, nq),
        in_specs=[
            vec, vec, vec, vec,
            pl.BlockSpec((1, width), lambda b, h, i: (0, 0)),
            pl.BlockSpec((tq, width), lambda b, h, i: (b * nq + i, h)),
            pl.BlockSpec((seq, width), lambda b, h, i: (b, heads + h)),
            pl.BlockSpec((seq, width), lambda b, h, i: (b, 2 * heads + h)),
        ],
        out_specs=pl.BlockSpec((tq, width), lambda b, h, i: (b * nq + i, h)),
        out_shape=jax.ShapeDtypeStruct((batch * seq, heads * width), BF16),
        scratch_shapes=[
            pltpu.VMEM((2, tq, LANES), F32),
            pltpu.VMEM((2, tq, LANES), F32),
            pltpu.VMEM((2, tq, LANES), F32),
            pltpu.VMEM((2, tq, width), F32),
            pltpu.VMEM((2, 2, tq, tq), F32),
            pltpu.VMEM((2, tq, tq), BF16),
        ],
        compiler_params=_params(3),
        name="diff_attention",
    )(lq1, lk1, lq2, lk2, subln_g, qkv, qkv, qkv)


def _merge_kernel(x_ref, a_ref, c_ref, wga_ref, wgc_ref, wa_ref, wc_ref, o_ref):
    x = x_ref[...]
    gate_a = jax.nn.sigmoid(_dot(x, wga_ref[...]))
    gate_c = jax.nn.sigmoid(_dot(x, wgc_ref[...]))
    y_a = _dot(a_ref[...], wa_ref[...])
    y_c = _dot(c_ref[...], wc_ref[...])
    o_ref[...] = (gate_a * y_a + gate_c * y_c).astype(o_ref.dtype)


def _merge(x, attn, yc, w, gate_offsets, wa, wc):
    name = "gated_merge"
    m, k = x.shape
    ka, kc = attn.shape[1], yc.shape[1]
    n = wa.shape[1]
    bm, bn = _block(m, TILES[name][0]), _block(n, TILES[name][1])
    return pl.pallas_call(
        _merge_kernel,
        grid=(m // bm, n // bn),
        in_specs=[
            _row_spec(bm, k), _row_spec(bm, ka), _row_spec(bm, kc),
            _col_spec(k, bn, gate_offsets[0]), _col_spec(k, bn, gate_offsets[1]),
            _col_spec(ka, bn, 0), _col_spec(kc, bn, 0),
        ],
        out_specs=pl.BlockSpec((bm, bn), lambda i, j: (i, j)),
        out_shape=jax.ShapeDtypeStruct((m, n), BF16),
        compiler_params=_params(2),
        name=name,
    )(x, attn, yc, w, w, wa, wc)


def _layer_norm_kernel(r_ref, g_ref, b_ref, *o_refs):
    r = r_ref[...]
    mu = jnp.mean(r, axis=-1, keepdims=True)
    rc = r - mu
    var = jnp.mean(rc * rc, axis=-1, keepdims=True)
    y = rc * lax.rsqrt(var + LN_EPS) * g_ref[...] + b_ref[...]
    for o_ref in o_refs:
        o_ref[...] = y.astype(o_ref.dtype)


def _layer_norm(r, g, b, out_dtypes, name):
    m, d = r.shape
    bm = _block(m, LN_ROWS)
    row = pl.BlockSpec((bm, d), lambda i: (i, 0))
    vec = pl.BlockSpec((1, d), lambda i: (0, 0))
    return pl.pallas_call(
        _layer_norm_kernel,
        grid=(m // bm,),
        in_specs=[row, vec, vec],
        out_specs=[row] * len(out_dtypes),
        out_shape=[jax.ShapeDtypeStruct((m, d), dt) for dt in out_dtypes],
        compiler_params=_params(1),
        name=name,
    )(r, g, b)


def _lambda_init(layer_idx):
    return 0.8 - 0.6 * math.exp(-0.3 * layer_idx)


def kernel(x, w_in, lambda_q1, lambda_k1, lambda_q2, lambda_k2, subln_g, conv_mix_w, w_attn_out,
           w_conv_out, w_o, ln1_g, ln1_b, w_up, ffn_conv_w, w_down, ln2_g, ln2_b):
    batch, seq, d_model = x.shape
    depth = w_in.shape[0]
    hd = lambda_q1.shape[-1]
    v_width = w_attn_out.shape[1]
    heads = v_width // (2 * hd)
    qk_width = heads * 2 * hd
    conv_width = conv_mix_w.shape[-1]
    alpha = (2.0 * depth) ** 0.25
    rows = batch * seq

    o_u = 2 * qk_width + v_width
    o_b = o_u + conv_width
    o_c = o_b + conv_width
    o_ga = o_c + conv_width
    o_gc = o_ga + d_model

    q_scale = LOG2E / math.sqrt(hd)
    col_scale = jnp.concatenate(
        [jnp.full((1, qk_width), q_scale, F32), jnp.ones((1, o_u - qk_width), F32)], axis=1)

    h = x.reshape(rows, d_model)
    for l in range(depth):
        lam_init = _lambda_init(l)
        h_bf = h.astype(BF16)
        w_in_bf = w_in[l].astype(BF16)

        qkv = _proj_scale(h_bf, w_in_bf, col_scale, "qkv_proj")
        attn = _diff_attention(
            qkv, lambda_q1[l][None], lambda_k1[l][None], lambda_q2[l][None], lambda_k2[l][None],
            subln_g[l][None], batch=batch, seq=seq, heads=heads, hd=hd, lam_init=lam_init)
        yc = _shortconv(h_bf, w_in_bf, (o_u, o_b, o_c), conv_mix_w[l], seq)
        merged = _merge(h_bf, attn, yc, w_in_bf, (o_ga, o_gc),
                        w_attn_out[l].astype(BF16), w_conv_out[l].astype(BF16))

        r1 = _proj_residual(merged, w_o[l].astype(BF16), h, alpha, "wo_proj")
        h1, h1_bf = _layer_norm(r1, ln1_g[l][None], ln1_b[l][None], (F32, BF16), "ln1")

        act = _ffn_up(h1_bf, w_up[l].astype(BF16), ffn_conv_w[l], seq)
        r2 = _proj_residual(act, w_down[l].astype(BF16), h1, alpha, "down_proj")
        (h,) = _layer_norm(r2, ln2_g[l][None], ln2_b[l][None], (F32,), "ln2")
    return h.reshape(batch, seq, d_model)
```

```python
import functools
import math

import jax
import jax.numpy as jnp
from jax import lax
from jax.experimental import pallas as pl
from jax.experimental.pallas import tpu as pltpu

LN_EPS = 1e-5
RMS_EPS = 1e-5
LOG2E = 1.4426950408889634
CONV_TAPS = 3
SUBLANES = 8
LANES = 128
MASK_VALUE = -1e30
VMEM_LIMIT_BYTES = 56 * 1024 * 1024

TILES = {
    "qkv_proj": (1024, 1024),
    "shortconv_proj": (1024, 256),
    "gated_merge": (1024, 256),
    "wo_proj": (1024, 512),
    "ffn_up_conv_act": (2048, 256),
    "down_proj": (512, 512),
}
ATTN_BLOCK = 512
SOFTMAX_ROWS = 64
LN_ROWS = 256

F32 = jnp.float32
BF16 = jnp.bfloat16


def _params(n_axes):
    return pltpu.CompilerParams(
        dimension_semantics=("arbitrary",) * n_axes,
        vmem_limit_bytes=VMEM_LIMIT_BYTES,
    )


def _block(dim, want):
    if dim <= want:
        return dim
    b = want
    while dim % b:
        b //= 2
    return b


def _dot(a, b):
    return jnp.dot(a, b, preferred_element_type=F32)


def _row_spec(bm, k, buffers=2):
    return pl.BlockSpec((bm, k), lambda i, j: (i, 0), pipeline_mode=pl.Buffered(buffers))


def _col_spec(rows, bn, offset):
    assert offset % bn == 0, (offset, bn)
    return pl.BlockSpec((rows, bn), lambda i, j: (0, offset // bn + j))


def _proj_scale_kernel(x_ref, w_ref, s_ref, o_ref):
    o_ref[...] = (_dot(x_ref[...], w_ref[...]) * s_ref[...]).astype(o_ref.dtype)


def _proj_scale(x, w, col_scale, name):
    m, k = x.shape
    n = col_scale.shape[1]
    bm, bn = _block(m, TILES[name][0]), _block(n, TILES[name][1])
    return pl.pallas_call(
        _proj_scale_kernel,
        grid=(m // bm, n // bn),
        in_specs=[_row_spec(bm, k), _col_spec(k, bn, 0), _col_spec(1, bn, 0)],
        out_specs=pl.BlockSpec((bm, bn), lambda i, j: (i, j)),
        out_shape=jax.ShapeDtypeStruct((m, n), BF16),
        compiler_params=_params(2),
        name=name,
    )(x, w, col_scale)


def _proj_residual_kernel(x_ref, w_ref, r_ref, o_ref, *, alpha):
    o_ref[...] = alpha * r_ref[...] + _dot(x_ref[...], w_ref[...])


def _proj_residual(x, w, resid, alpha, name):
    m, k = x.shape
    n = w.shape[1]
    bm, bn = _block(m, TILES[name][0]), _block(n, TILES[name][1])
    return pl.pallas_call(
        functools.partial(_proj_residual_kernel, alpha=alpha),
        grid=(m // bm, n // bn),
        in_specs=[_row_spec(bm, k), _col_spec(k, bn, 0), pl.BlockSpec((bm, bn), lambda i, j: (i, j))],
        out_specs=pl.BlockSpec((bm, bn), lambda i, j: (i, j)),
        out_shape=jax.ShapeDtypeStruct((m, n), F32),
        compiler_params=_params(2),
        name=name,
    )(x, w, resid)


def _conv_body(z, w_ref):
    z1 = pltpu.roll(z, 1, 0)
    z2 = pltpu.roll(z, 2, 0)
    return w_ref[0:1, :] * z2 + w_ref[1:2, :] * z1 + w_ref[2:3, :] * z


def _conv_head(z_head, prev, w_ref):
    row = lax.broadcasted_iota(jnp.int32, z_head.shape, 0)
    z1 = jnp.where(row == 0, prev[SUBLANES - 1:SUBLANES, :], pltpu.roll(z_head, 1, 0))
    z2 = jnp.where(row == 0, prev[SUBLANES - 2:SUBLANES - 1, :],
                   jnp.where(row == 1, prev[SUBLANES - 1:SUBLANES, :], pltpu.roll(z_head, 2, 0)))
    return w_ref[0:1, :] * z2 + w_ref[1:2, :] * z1 + w_ref[2:3, :] * z_head


def _halo(carry_ref, idx, new_tail, seq_start):
    @pl.when(seq_start)
    def _():
        for i in idx[1:]:
            carry_ref[idx[0], i] = jnp.zeros(carry_ref.shape[2:], F32)

    prev = [carry_ref[idx[0], i] for i in idx[1:]]
    for i, t in zip(idx[1:], new_tail):
        carry_ref[idx[0], i] = t
    return prev


def _shortconv_kernel(x_ref, wu_ref, wb_ref, wc_ref, cw_ref, o_ref, carry_ref, *, blocks_per_seq):
    i, j = pl.program_id(0), pl.program_id(1)
    x = x_ref[...]
    u = _dot(x, wu_ref[...])
    g_b = _dot(x, wb_ref[...])
    g_c = _dot(x, wc_ref[...])
    v = g_c * u
    bm = v.shape[0]
    (prev,) = _halo(carry_ref, (j, 0), [v[bm - SUBLANES:, :]], i % blocks_per_seq == 0)
    o_ref[...] = (g_b * _conv_body(v, cw_ref)).astype(o_ref.dtype)
    head = _conv_head(v[:SUBLANES, :], prev, cw_ref)
    o_ref[0:SUBLANES, :] = (g_b[:SUBLANES, :] * head).astype(o_ref.dtype)


def _shortconv(x, w, offsets, conv_w, seq):
    name = "shortconv_proj"
    m, k = x.shape
    n = conv_w.shape[1]
    bm, bn = _block(seq, TILES[name][0]), _block(n, TILES[name][1])
    return pl.pallas_call(
        functools.partial(_shortconv_kernel, blocks_per_seq=seq // bm),
        grid=(m // bm, n // bn),
        in_specs=[_row_spec(bm, k)] + [_col_spec(k, bn, o) for o in offsets] + [_col_spec(CONV_TAPS, bn, 0)],
        out_specs=pl.BlockSpec((bm, bn), lambda i, j: (i, j)),
        out_shape=jax.ShapeDtypeStruct((m, n), BF16),
        scratch_shapes=[pltpu.VMEM((n // bn, 1, SUBLANES, bn), F32)],
        compiler_params=_params(2),
        name=name,
    )(x, w, w, w, conv_w)


def _silu(x):
    return x * jax.nn.sigmoid(x)


def _ffn_up_kernel(h_ref, wg_ref, wv_ref, cg_ref, cv_ref, o_ref, carry_ref, *, blocks_per_seq):
    i, j = pl.program_id(0), pl.program_id(1)
    h = h_ref[...]
    z_g = _dot(h, wg_ref[...])
    z_v = _dot(h, wv_ref[...])
    bm = z_g.shape[0]
    prev_g, prev_v = _halo(carry_ref, (j, 0, 1),
                           [z_g[bm - SUBLANES:, :], z_v[bm - SUBLANES:, :]],
                           i % blocks_per_seq == 0)
    o_ref[...] = (_silu(_conv_body(z_g, cg_ref)) * _conv_body(z_v, cv_ref)).astype(o_ref.dtype)
    head_g = _conv_head(z_g[:SUBLANES, :], prev_g, cg_ref)
    head_v = _conv_head(z_v[:SUBLANES, :], prev_v, cv_ref)
    o_ref[0:SUBLANES, :] = (_silu(head_g) * head_v).astype(o_ref.dtype)


def _ffn_up(h, w_up, conv_w, seq):
    name = "ffn_up_conv_act"
    m, k = h.shape
    n = w_up.shape[1] // 2
    bm, bn = _block(seq, TILES[name][0]), _block(n, TILES[name][1])
    return pl.pallas_call(
        functools.partial(_ffn_up_kernel, blocks_per_seq=seq // bm),
        grid=(m // bm, n // bn),
        in_specs=[_row_spec(bm, k), _col_spec(k, bn, 0), _col_spec(k, bn, n),
                  _col_spec(CONV_TAPS, bn, 0), _col_spec(CONV_TAPS, bn, n)],
        out_specs=pl.BlockSpec((bm, bn), lambda i, j: (i, j)),
        out_shape=jax.ShapeDtypeStruct((m, n), BF16),
        scratch_shapes=[pltpu.VMEM((n // bn, 2, SUBLANES, bn), F32)],
        compiler_params=_params(2),
        name=name,
    )(h, w_up, w_up, conv_w, conv_w)


def _attn_kernel(lq1_ref, lk1_ref, lq2_ref, lk2_ref, g_ref, q_ref, k_ref, v_ref, o_ref,
                 m_ref, l_ref, alpha_ref, acc_ref, s_ref, p_ref, *, hd, lam_init):
    qi = pl.program_id(2)
    t = q_ref.shape[0]
    n_lane_tiles = t // LANES

    m_ref[...] = jnp.full(m_ref.shape, MASK_VALUE, F32)
    l_ref[...] = jnp.zeros(l_ref.shape, F32)
    acc_ref[...] = jnp.zeros(acc_ref.shape, F32)

    def scores(kb, slot):
        k = k_ref[pl.ds(pl.multiple_of(kb * t, t), t), :]
        for c in range(2):
            s_ref[slot, c] = lax.dot_general(
                q_ref[:, c * hd:(c + 1) * hd], k[:, c * hd:(c + 1) * hd], (((1,), (1,)), ((), ())),
                preferred_element_type=F32)

    def softmax_rows(slot, c, r, diagonal):
        rows = pl.ds(r, SOFTMAX_ROWS)
        jd = r // LANES
        n_tiles = jd + 1 if diagonal else n_lane_tiles
        tiles = [s_ref[slot, c, rows, j * LANES:(j + 1) * LANES] for j in range(n_tiles)]
        if diagonal:
            tri = (lax.broadcasted_iota(jnp.int32, (SOFTMAX_ROWS, LANES), 1)
                   <= lax.broadcasted_iota(jnp.int32, (SOFTMAX_ROWS, LANES), 0) + r % LANES)
            tiles[jd] = jnp.where(tri, tiles[jd], MASK_VALUE)
        m_old = m_ref[c, rows, :]
        m_new = jnp.maximum(m_old, jnp.max(functools.reduce(jnp.maximum, tiles), axis=-1, keepdims=True))
        alpha = jnp.exp2(m_old - m_new)
        p_tiles = [jnp.exp2(tile - m_new) for tile in tiles]
        l_ref[c, rows, :] = alpha * l_ref[c, rows, :] + functools.reduce(jnp.add, p_tiles)
        p_tiles += [jnp.zeros((SOFTMAX_ROWS, LANES), F32)] * (n_lane_tiles - n_tiles)
        p_ref[slot, c, rows, :] = jnp.concatenate(p_tiles, axis=1).astype(BF16)
        alpha_ref[slot, c, rows, :] = alpha
        m_ref[c, rows, :] = m_new

    def accumulate(kb, slot, diagonal):
        v = v_ref[pl.ds(pl.multiple_of(kb * t, t), t), :]
        for c in range(2):
            for r in range(0, t, SOFTMAX_ROWS):
                softmax_rows(slot, c, r, diagonal)
            alpha = jnp.concatenate([alpha_ref[slot, c]] * (acc_ref.shape[-1] // LANES), axis=1)
            acc_ref[c] = alpha * acc_ref[c] + _dot(p_ref[slot, c], v)

    def pair(i, carry):
        kb = 2 * i
        scores(kb + 1, 1)
        accumulate(kb, 0, False)
        scores(kb + 2, 0)
        accumulate(kb + 1, 1, False)
        return carry

    scores(0, 0)
    lax.fori_loop(0, qi // 2, pair, 0)

    @pl.when(qi % 2 == 0)
    def _():
        accumulate(qi, 0, True)

    @pl.when(qi % 2 == 1)
    def _():
        scores(qi, 1)
        accumulate(qi - 1, 0, False)
        accumulate(qi, 1, True)

    lam = (jnp.exp(jnp.sum(lq1_ref[...] * lk1_ref[...], axis=-1, keepdims=True))
           - jnp.exp(jnp.sum(lq2_ref[...] * lk2_ref[...], axis=-1, keepdims=True))
           + lam_init)
    l1 = jnp.sum(l_ref[0], axis=-1, keepdims=True)
    l2 = jnp.sum(l_ref[1], axis=-1, keepdims=True)
    o = acc_ref[0] / l1 - lam * (acc_ref[1] / l2)
    ms = jnp.mean(o * o, axis=-1, keepdims=True)
    o = o * lax.rsqrt(ms + RMS_EPS) * g_ref[...] * (1.0 - lam_init)
    o_ref[...] = o.astype(o_ref.dtype)


def _diff_attention(qkv, lq1, lk1, lq2, lk2, subln_g, *, batch, seq, heads, hd, lam_init):
    tq = _block(seq, ATTN_BLOCK)
    assert tq % LANES == 0, "query blocks are processed in LANES-row chunks"
    nq = seq // tq
    width = 2 * hd
    vec = pl.BlockSpec((1, hd), lambda b, h, i: (0, 0))
    return pl.pallas_call(
        functools.partial(_attn_kernel, hd=hd, lam_init=lam_init),
        grid=(batch, heads, nq),
        in_specs=[
            vec, vec, vec, vec,
            pl.BlockSpec((1, width), lambda b, h, i: (0, 0)),
            pl.BlockSpec((tq, width), lambda b, h, i: (b * nq + i, h)),
            pl.BlockSpec((seq, width), lambda b, h, i: (b, heads + h)),
            pl.BlockSpec((seq, width), lambda b, h, i: (b, 2 * heads + h)),
        ],
        out_specs=pl.BlockSpec((tq, width), lambda b, h, i: (b * nq + i, h)),
        out_shape=jax.ShapeDtypeStruct((batch * seq, heads * width), BF16),
        scratch_shapes=[
            pltpu.VMEM((2, tq, LANES), F32),
            pltpu.VMEM((2, tq, LANES), F32),
            pltpu.VMEM((2, 2, tq, LANES), F32),
            pltpu.VMEM((2, tq, width), F32),
            pltpu.VMEM((2, 2, tq, tq), F32),
            pltpu.VMEM((2, 2, tq, tq), BF16),
        ],
        compiler_params=_params(3),
        name="diff_attention",
    )(lq1, lk1, lq2, lk2, subln_g, qkv, qkv, qkv)


def _merge_kernel(x_ref, a_ref, c_ref, wga_ref, wgc_ref, wa_ref, wc_ref, o_ref):
    x = x_ref[...]
    gate_a = jax.nn.sigmoid(_dot(x, wga_ref[...]))
    gate_c = jax.nn.sigmoid(_dot(x, wgc_ref[...]))
    y_a = _dot(a_ref[...], wa_ref[...])
    y_c = _dot(c_ref[...], wc_ref[...])
    o_ref[...] = (gate_a * y_a + gate_c * y_c).astype(o_ref.dtype)


def _merge(x, attn, yc, w, gate_offsets, wa, wc):
    name = "gated_merge"
    m, k = x.shape
    ka, kc = attn.shape[1], yc.shape[1]
    n = wa.shape[1]
    bm, bn = _block(m, TILES[name][0]), _block(n, TILES[name][1])
    return pl.pallas_call(
        _merge_kernel,
        grid=(m // bm, n // bn),
        in_specs=[
            _row_spec(bm, k, 1), _row_spec(bm, ka, 1), _row_spec(bm, kc, 1),
            _col_spec(k, bn, gate_offsets[0]), _col_spec(k, bn, gate_offsets[1]),
            _col_spec(ka, bn, 0), _col_spec(kc, bn, 0),
        ],
        out_specs=pl.BlockSpec((bm, bn), lambda i, j: (i, j)),
        out_shape=jax.ShapeDtypeStruct((m, n), BF16),
        compiler_params=_params(2),
        name=name,
    )(x, attn, yc, w, w, wa, wc)


def _layer_norm_kernel(r_ref, g_ref, b_ref, *o_refs):
    r = r_ref[...]
    mu = jnp.mean(r, axis=-1, keepdims=True)
    rc = r - mu
    var = jnp.mean(rc * rc, axis=-1, keepdims=True)
    y = rc * lax.rsqrt(var + LN_EPS) * g_ref[...] + b_ref[...]
    for o_ref in o_refs:
        o_ref[...] = y.astype(o_ref.dtype)


def _layer_norm(r, g, b, out_dtypes, name):
    m, d = r.shape
    bm = _block(m, LN_ROWS)
    row = pl.BlockSpec((bm, d), lambda i: (i, 0))
    vec = pl.BlockSpec((1, d), lambda i: (0, 0))
    return pl.pallas_call(
        _layer_norm_kernel,
        grid=(m // bm,),
        in_specs=[row, vec, vec],
        out_specs=[row] * len(out_dtypes),
        out_shape=[jax.ShapeDtypeStruct((m, d), dt) for dt in out_dtypes],
        compiler_params=_params(1),
        name=name,
    )(r, g, b)


def _lambda_init(layer_idx):
    return 0.8 - 0.6 * math.exp(-0.3 * layer_idx)


def kernel(x, w_in, lambda_q1, lambda_k1, lambda_q2, lambda_k2, subln_g, conv_mix_w, w_attn_out,
           w_conv_out, w_o, ln1_g, ln1_b, w_up, ffn_conv_w, w_down, ln2_g, ln2_b):
    batch, seq, d_model = x.shape
    depth = w_in.shape[0]
    hd = lambda_q1.shape[-1]
    v_width = w_attn_out.shape[1]
    heads = v_width // (2 * hd)
    qk_width = heads * 2 * hd
    conv_width = conv_mix_w.shape[-1]
    alpha = (2.0 * depth) ** 0.25
    rows = batch * seq

    o_u = 2 * qk_width + v_width
    o_b = o_u + conv_width
    o_c = o_b + conv_width
    o_ga = o_c + conv_width
    o_gc = o_ga + d_model

    q_scale = LOG2E / math.sqrt(hd)
    col_scale = jnp.concatenate(
        [jnp.full((1, qk_width), q_scale, F32), jnp.ones((1, o_u - qk_width), F32)], axis=1)

    h = x.reshape(rows, d_model)
    for l in range(depth):
        lam_init = _lambda_init(l)
        h_bf = h.astype(BF16)
        w_in_bf = w_in[l].astype(BF16)

        qkv = _proj_scale(h_bf, w_in_bf, col_scale, "qkv_proj")
        attn = _diff_attention(
            qkv, lambda_q1[l][None], lambda_k1[l][None], lambda_q2[l][None], lambda_k2[l][None],
            subln_g[l][None], batch=batch, seq=seq, heads=heads, hd=hd, lam_init=lam_init)
        yc = _shortconv(h_bf, w_in_bf, (o_u, o_b, o_c), conv_mix_w[l], seq)
        merged = _merge(h_bf, attn, yc, w_in_bf, (o_ga, o_gc),
                        w_attn_out[l].astype(BF16), w_conv_out[l].astype(BF16))

        r1 = _proj_residual(merged, w_o[l].astype(BF16), h, alpha, "wo_proj")
        h1, h1_bf = _layer_norm(r1, ln1_g[l][None], ln1_b[l][None], (F32, BF16), "ln1")

        act = _ffn_up(h1_bf, w_up[l].astype(BF16), ffn_conv_w[l], seq)
        r2 = _proj_residual(act, w_down[l].astype(BF16), h1, alpha, "down_proj")
        (h,) = _layer_norm(r2, ln2_g[l][None], ln2_b[l][None], (F32,), "ln2")
    return h.reshape(batch, seq, d_model)
```

```python
import functools
import math

import jax
import jax.numpy as jnp
from jax import lax
from jax.experimental import pallas as pl
from jax.experimental.pallas import tpu as pltpu

LN_EPS = 1e-5
RMS_EPS = 1e-5
LOG2E = 1.4426950408889634
CONV_TAPS = 3
SUBLANES = 8
LANES = 128
MASK_VALUE = -1e30
VMEM_LIMIT_BYTES = 56 * 1024 * 1024

TILES = {
    "qkv_proj": (1024, 1024),
    "shortconv_proj": (1024, 256),
    "gated_merge": (512, 512),
    "wo_proj": (1024, 512),
    "ffn_up_conv_act": (2048, 256),
    "down_proj": (512, 512),
}
ATTN_BLOCK = 512
SOFTMAX_ROWS = 64
LN_ROWS = 256

F32 = jnp.float32
BF16 = jnp.bfloat16


def _params(n_axes):
    return pltpu.CompilerParams(
        dimension_semantics=("arbitrary",) * n_axes,
        vmem_limit_bytes=VMEM_LIMIT_BYTES,
    )


def _block(dim, want):
    if dim <= want:
        return dim
    b = want
    while dim % b:
        b //= 2
    return b


def _dot(a, b):
    return lax.dot_general(a, b, (((1,), (0,)), ((), ())), preferred_element_type=F32)


def _row_spec(bm, k, buffers=2):
    return pl.BlockSpec((bm, k), lambda i, j: (i, 0), pipeline_mode=pl.Buffered(buffers))


def _col_spec(rows, bn, offset):
    assert offset % bn == 0, (offset, bn)
    return pl.BlockSpec((rows, bn), lambda i, j: (0, offset // bn + j))


def _proj_scale_kernel(x_ref, w_ref, s_ref, o_ref):
    o_ref[...] = (_dot(x_ref[...], w_ref[...]) * s_ref[...]).astype(o_ref.dtype)


def _proj_scale(x, w, col_scale, name):
    m, k = x.shape
    n = col_scale.shape[1]
    bm, bn = _block(m, TILES[name][0]), _block(n, TILES[name][1])
    return pl.pallas_call(
        _proj_scale_kernel,
        grid=(m // bm, n // bn),
        in_specs=[_row_spec(bm, k), _col_spec(k, bn, 0), _col_spec(1, bn, 0)],
        out_specs=pl.BlockSpec((bm, bn), lambda i, j: (i, j)),
        out_shape=jax.ShapeDtypeStruct((m, n), BF16),
        compiler_params=_params(2),
        name=name,
    )(x, w, col_scale)


def _proj_residual_kernel(x_ref, w_ref, r_ref, o_ref, *, alpha):
    o_ref[...] = alpha * r_ref[...] + _dot(x_ref[...], w_ref[...])


def _proj_residual(x, w, resid, alpha, name):
    m, k = x.shape
    n = w.shape[1]
    bm, bn = _block(m, TILES[name][0]), _block(n, TILES[name][1])
    return pl.pallas_call(
        functools.partial(_proj_residual_kernel, alpha=alpha),
        grid=(m // bm, n // bn),
        in_specs=[_row_spec(bm, k), _col_spec(k, bn, 0), pl.BlockSpec((bm, bn), lambda i, j: (i, j))],
        out_specs=pl.BlockSpec((bm, bn), lambda i, j: (i, j)),
        out_shape=jax.ShapeDtypeStruct((m, n), F32),
        compiler_params=_params(2),
        name=name,
    )(x, w, resid)


def _conv_body(z, w_ref):
    z1 = pltpu.roll(z, 1, 0)
    z2 = pltpu.roll(z, 2, 0)
    return w_ref[0:1, :] * z2 + w_ref[1:2, :] * z1 + w_ref[2:3, :] * z


def _conv_head(z_head, prev, w_ref):
    row = lax.broadcasted_iota(jnp.int32, z_head.shape, 0)
    z1 = jnp.where(row == 0, prev[SUBLANES - 1:SUBLANES, :], pltpu.roll(z_head, 1, 0))
    z2 = jnp.where(row == 0, prev[SUBLANES - 2:SUBLANES - 1, :],
                   jnp.where(row == 1, prev[SUBLANES - 1:SUBLANES, :], pltpu.roll(z_head, 2, 0)))
    return w_ref[0:1, :] * z2 + w_ref[1:2, :] * z1 + w_ref[2:3, :] * z_head


def _halo(carry_ref, idx, new_tail, seq_start):
    @pl.when(seq_start)
    def _():
        for i in idx[1:]:
            carry_ref[idx[0], i] = jnp.zeros(carry_ref.shape[2:], F32)

    prev = [carry_ref[idx[0], i] for i in idx[1:]]
    for i, t in zip(idx[1:], new_tail):
        carry_ref[idx[0], i] = t
    return prev


def _shortconv_kernel(x_ref, wu_ref, wb_ref, wc_ref, cw_ref, o_ref, carry_ref, *, blocks_per_seq):
    i, j = pl.program_id(0), pl.program_id(1)
    x = x_ref[...]
    u = _dot(x, wu_ref[...])
    g_b = _dot(x, wb_ref[...])
    g_c = _dot(x, wc_ref[...])
    v = g_c * u
    bm = v.shape[0]
    (prev,) = _halo(carry_ref, (j, 0), [v[bm - SUBLANES:, :]], i % blocks_per_seq == 0)
    o_ref[...] = (g_b * _conv_body(v, cw_ref)).astype(o_ref.dtype)
    head = _conv_head(v[:SUBLANES, :], prev, cw_ref)
    o_ref[0:SUBLANES, :] = (g_b[:SUBLANES, :] * head).astype(o_ref.dtype)


def _shortconv(x, w, offsets, conv_w, seq):
    name = "shortconv_proj"
    m, k = x.shape
    n = conv_w.shape[1]
    bm, bn = _block(seq, TILES[name][0]), _block(n, TILES[name][1])
    return pl.pallas_call(
        functools.partial(_shortconv_kernel, blocks_per_seq=seq // bm),
        grid=(m // bm, n // bn),
        in_specs=[_row_spec(bm, k)] + [_col_spec(k, bn, o) for o in offsets] + [_col_spec(CONV_TAPS, bn, 0)],
        out_specs=pl.BlockSpec((bm, bn), lambda i, j: (i, j)),
        out_shape=jax.ShapeDtypeStruct((m, n), BF16),
        scratch_shapes=[pltpu.VMEM((n // bn, 1, SUBLANES, bn), F32)],
        compiler_params=_params(2),
        name=name,
    )(x, w, w, w, conv_w)


def _silu(x):
    return x * jax.nn.sigmoid(x)


def _ffn_up_kernel(h_ref, wg_ref, wv_ref, cg_ref, cv_ref, o_ref, carry_ref, *, blocks_per_seq):
    i, j = pl.program_id(0), pl.program_id(1)
    h = h_ref[...]
    z_g = _dot(h, wg_ref[...].astype(BF16))
    z_v = _dot(h, wv_ref[...].astype(BF16))
    bm = z_g.shape[0]
    prev_g, prev_v = _halo(carry_ref, (j, 0, 1),
                           [z_g[bm - SUBLANES:, :], z_v[bm - SUBLANES:, :]],
                           i % blocks_per_seq == 0)
    o_ref[...] = (_silu(_conv_body(z_g, cg_ref)) * _conv_body(z_v, cv_ref)).astype(o_ref.dtype)
    head_g = _conv_head(z_g[:SUBLANES, :], prev_g, cg_ref)
    head_v = _conv_head(z_v[:SUBLANES, :], prev_v, cv_ref)
    o_ref[0:SUBLANES, :] = (_silu(head_g) * head_v).astype(o_ref.dtype)


def _ffn_up(h, w_up, conv_w, seq):
    name = "ffn_up_conv_act"
    m, k = h.shape
    n = w_up.shape[1] // 2
    bm, bn = _block(seq, TILES[name][0]), _block(n, TILES[name][1])
    return pl.pallas_call(
        functools.partial(_ffn_up_kernel, blocks_per_seq=seq // bm),
        grid=(m // bm, n // bn),
        in_specs=[_row_spec(bm, k, 1), _col_spec(k, bn, 0), _col_spec(k, bn, n),
                  _col_spec(CONV_TAPS, bn, 0), _col_spec(CONV_TAPS, bn, n)],
        out_specs=pl.BlockSpec((bm, bn), lambda i, j: (i, j)),
        out_shape=jax.ShapeDtypeStruct((m, n), BF16),
        scratch_shapes=[pltpu.VMEM((n // bn, 2, SUBLANES, bn), F32)],
        compiler_params=_params(2),
        name=name,
    )(h, w_up, w_up, conv_w, conv_w)


def _attn_kernel(lq1_ref, lk1_ref, lq2_ref, lk2_ref, g_ref, q_ref, k_ref, v_ref, o_ref,
                 m_ref, l_ref, alpha_ref, acc_ref, s_ref, p_ref, *, hd, lam_init):
    qi = pl.program_id(2)
    t = q_ref.shape[0]
    n_lane_tiles = t // LANES

    m_ref[...] = jnp.full(m_ref.shape, MASK_VALUE, F32)
    l_ref[...] = jnp.zeros(l_ref.shape, F32)
    acc_ref[...] = jnp.zeros(acc_ref.shape, F32)

    def scores(kb, slot):
        k = k_ref[pl.ds(pl.multiple_of(kb * t, t), t), :]
        for c in range(2):
            s_ref[slot, c] = lax.dot_general(
                q_ref[:, c * hd:(c + 1) * hd], k[:, c * hd:(c + 1) * hd], (((1,), (1,)), ((), ())),
                preferred_element_type=F32)

    def softmax_rows(slot, c, r, diagonal):
        rows = pl.ds(r, SOFTMAX_ROWS)
        jd = r // LANES
        n_tiles = jd + 1 if diagonal else n_lane_tiles
        tiles = [s_ref[slot, c, rows, j * LANES:(j + 1) * LANES] for j in range(n_tiles)]
        if diagonal:
            tri = (lax.broadcasted_iota(jnp.int32, (SOFTMAX_ROWS, LANES), 1)
                   <= lax.broadcasted_iota(jnp.int32, (SOFTMAX_ROWS, LANES), 0) + r % LANES)
            tiles[jd] = jnp.where(tri, tiles[jd], MASK_VALUE)
        m_old = m_ref[c, rows, :]
        m_new = jnp.maximum(m_old, jnp.max(functools.reduce(jnp.maximum, tiles), axis=-1, keepdims=True))
        alpha = jnp.exp2(m_old - m_new)
        p_tiles = [jnp.exp2(tile - m_new) for tile in tiles]
        l_ref[c, rows, :] = alpha * l_ref[c, rows, :] + functools.reduce(jnp.add, p_tiles)
        p_tiles += [jnp.zeros((SOFTMAX_ROWS, LANES), F32)] * (n_lane_tiles - n_tiles)
        p_ref[slot, c, rows, :] = jnp.concatenate(p_tiles, axis=1).astype(BF16)
        alpha_ref[slot, c, rows, :] = alpha
        m_ref[c, rows, :] = m_new

    def accumulate(kb, slot, diagonal):
        v = v_ref[pl.ds(pl.multiple_of(kb * t, t), t), :]
        for c in range(2):
            for r in range(0, t, SOFTMAX_ROWS):
                softmax_rows(slot, c, r, diagonal)
            alpha = jnp.concatenate([alpha_ref[slot, c]] * (acc_ref.shape[-1] // LANES), axis=1)
            acc_ref[c] = alpha * acc_ref[c] + _dot(p_ref[slot, c], v)

    def pair(i, carry):
        kb = 2 * i
        scores(kb + 1, 1)
        accumulate(kb, 0, False)
        scores(kb + 2, 0)
        accumulate(kb + 1, 1, False)
        return carry

    scores(0, 0)
    lax.fori_loop(0, qi // 2, pair, 0)

    @pl.when(qi % 2 == 0)
    def _():
        accumulate(qi, 0, True)

    @pl.when(qi % 2 == 1)
    def _():
        scores(qi, 1)
        accumulate(qi - 1, 0, False)
        accumulate(qi, 1, True)

    lam = (jnp.exp(jnp.sum(lq1_ref[...] * lk1_ref[...], axis=-1, keepdims=True))
           - jnp.exp(jnp.sum(lq2_ref[...] * lk2_ref[...], axis=-1, keepdims=True))
           + lam_init)
    l1 = jnp.sum(l_ref[0], axis=-1, keepdims=True)
    l2 = jnp.sum(l_ref[1], axis=-1, keepdims=True)
    o = acc_ref[0] / l1 - lam * (acc_ref[1] / l2)
    ms = jnp.mean(o * o, axis=-1, keepdims=True)
    o = o * lax.rsqrt(ms + RMS_EPS) * g_ref[...] * (1.0 - lam_init)
    o_ref[...] = o.astype(o_ref.dtype)


def _diff_attention(qkv, lq1, lk1, lq2, lk2, subln_g, *, batch, seq, heads, hd, lam_init):
    tq = _block(seq, ATTN_BLOCK)
    assert tq % LANES == 0, "query blocks are processed in LANES-row chunks"
    nq = seq // tq
    width = 2 * hd
    vec = pl.BlockSpec((1, hd), lambda b, h, i: (0, 0))
    return pl.pallas_call(
        functools.partial(_attn_kernel, hd=hd, lam_init=lam_init),
        grid=(batch, heads, nq),
        in_specs=[
            vec, vec, vec, vec,
            pl.BlockSpec((1, width), lambda b, h, i: (0, 0)),
            pl.BlockSpec((tq, width), lambda b, h, i: (b * nq + i, h)),
            pl.BlockSpec((seq, width), lambda b, h, i: (b, heads + h)),
            pl.BlockSpec((seq, width), lambda b, h, i: (b, 2 * heads + h)),
        ],
        out_specs=pl.BlockSpec((tq, width), lambda b, h, i: (b * nq + i, h)),
        out_shape=jax.ShapeDtypeStruct((batch * seq, heads * width), BF16),
        scratch_shapes=[
            pltpu.VMEM((2, tq, LANES), F32),
            pltpu.VMEM((2, tq, LANES), F32),
            pltpu.VMEM((2, 2, tq, LANES), F32),
            pltpu.VMEM((2, tq, width), F32),
            pltpu.VMEM((2, 2, tq, tq), F32),
            pltpu.VMEM((2, 2, tq, tq), BF16),
        ],
        compiler_params=_params(3),
        name="diff_attention",
    )(lq1, lk1, lq2, lk2, subln_g, qkv, qkv, qkv)


def _merge_kernel(x_ref, a_ref, c_ref, wga_ref, wgc_ref, wa_ref, wc_ref, o_ref):
    x = x_ref[...]
    gate_a = jax.nn.sigmoid(_dot(x, wga_ref[...]))
    gate_c = jax.nn.sigmoid(_dot(x, wgc_ref[...]))
    y_a = _dot(a_ref[...], wa_ref[...])
    y_c = _dot(c_ref[...], wc_ref[...])
    o_ref[...] = (gate_a * y_a + gate_c * y_c).astype(o_ref.dtype)


def _merge(x, attn, yc, w, gate_offsets, wa, wc):
    name = "gated_merge"
    m, k = x.shape
    ka, kc = attn.shape[1], yc.shape[1]
    n = wa.shape[1]
    bm, bn = _block(m, TILES[name][0]), _block(n, TILES[name][1])
    return pl.pallas_call(
        _merge_kernel,
        grid=(m // bm, n // bn),
        in_specs=[
            _row_spec(bm, k), _row_spec(bm, ka), _row_spec(bm, kc),
            _col_spec(k, bn, gate_offsets[0]), _col_spec(k, bn, gate_offsets[1]),
            _col_spec(ka, bn, 0), _col_spec(kc, bn, 0),
        ],
        out_specs=pl.BlockSpec((bm, bn), lambda i, j: (i, j)),
        out_shape=jax.ShapeDtypeStruct((m, n), BF16),
        compiler_params=_params(2),
        name=name,
    )(x, attn, yc, w, w, wa, wc)


def _layer_norm_kernel(r_ref, g_ref, b_ref, *o_refs):
    r = r_ref[...]
    mu = jnp.mean(r, axis=-1, keepdims=True)
    rc = r - mu
    var = jnp.mean(rc * rc, axis=-1, keepdims=True)
    y = rc * lax.rsqrt(var + LN_EPS) * g_ref[...] + b_ref[...]
    for o_ref in o_refs:
        o_ref[...] = y.astype(o_ref.dtype)


def _layer_norm(r, g, b, out_dtypes, name):
    m, d = r.shape
    bm = _block(m, LN_ROWS)
    row = pl.BlockSpec((bm, d), lambda i: (i, 0))
    vec = pl.BlockSpec((1, d), lambda i: (0, 0))
    return pl.pallas_call(
        _layer_norm_kernel,
        grid=(m // bm,),
        in_specs=[row, vec, vec],
        out_specs=[row] * len(out_dtypes),
        out_shape=[jax.ShapeDtypeStruct((m, d), dt) for dt in out_dtypes],
        compiler_params=_params(1),
        name=name,
    )(r, g, b)


def _lambda_init(layer_idx):
    return 0.8 - 0.6 * math.exp(-0.3 * layer_idx)


def kernel(x, w_in, lambda_q1, lambda_k1, lambda_q2, lambda_k2, subln_g, conv_mix_w, w_attn_out,
           w_conv_out, w_o, ln1_g, ln1_b, w_up, ffn_conv_w, w_down, ln2_g, ln2_b):
    batch, seq, d_model = x.shape
    depth = w_in.shape[0]
    hd = lambda_q1.shape[-1]
    v_width = w_attn_out.shape[1]
    heads = v_width // (2 * hd)
    qk_width = heads * 2 * hd
    conv_width = conv_mix_w.shape[-1]
    alpha = (2.0 * depth) ** 0.25
    rows = batch * seq

    o_u = 2 * qk_width + v_width
    o_b = o_u + conv_width
    o_c = o_b + conv_width
    o_ga = o_c + conv_width
    o_gc = o_ga + d_model

    q_scale = LOG2E / math.sqrt(hd)
    col_scale = jnp.concatenate(
        [jnp.full((1, qk_width), q_scale, F32), jnp.ones((1, o_u - qk_width), F32)], axis=1)

    h = x.reshape(rows, d_model)
    for l in range(depth):
        lam_init = _lambda_init(l)
        h_bf = h.astype(BF16)
        w_in_bf = w_in[l].astype(BF16)

        qkv = _proj_scale(h_bf, w_in_bf, col_scale, "qkv_proj")
        attn = _diff_attention(
            qkv, lambda_q1[l][None], lambda_k1[l][None], lambda_q2[l][None], lambda_k2[l][None],
            subln_g[l][None], batch=batch, seq=seq, heads=heads, hd=hd, lam_init=lam_init)
        yc = _shortconv(h_bf, w_in_bf, (o_u, o_b, o_c), conv_mix_w[l], seq)
        merged = _merge(h_bf, attn, yc, w_in_bf, (o_ga, o_gc),
                        w_attn_out[l].astype(BF16), w_conv_out[l].astype(BF16))

        r1 = _proj_residual(merged, w_o[l].astype(BF16), h, alpha, "wo_proj")
        h1, h1_bf = _layer_norm(r1, ln1_g[l][None], ln1_b[l][None], (F32, BF16), "ln1")

        act = _ffn_up(h1_bf, w_up[l], ffn_conv_w[l], seq)
        r2 = _proj_residual(act, w_down[l].astype(BF16), h1, alpha, "down_proj")
        (h,) = _layer_norm(r2, ln2_g[l][None], ln2_b[l][None], (F32,), "ln2")
    return h.reshape(batch, seq, d_model)
```

```python
import functools
import math

import jax
import jax.numpy as jnp
from jax import lax
from jax.experimental import pallas as pl
from jax.experimental.pallas import tpu as pltpu

LN_EPS = 1e-5
RMS_EPS = 1e-5
LOG2E = 1.4426950408889634
CONV_TAPS = 3
SUBLANES = 8
LANES = 128
MASK_VALUE = -1e30
VMEM_LIMIT_BYTES = 56 * 1024 * 1024

TILES = {
    "qkv_proj": (512, 1024),
    "shortconv_proj": (1024, 256),
    "gated_merge": (512, 512),
    "wo_proj": (1024, 512),
    "ffn_up_conv_act": (2048, 256),
    "down_proj": (512, 512),
}
ATTN_BLOCK = 512
SOFTMAX_ROWS = 64
LN_ROWS = 256

F32 = jnp.float32
BF16 = jnp.bfloat16


def _params(n_axes):
    return pltpu.CompilerParams(
        dimension_semantics=("arbitrary",) * n_axes,
        vmem_limit_bytes=VMEM_LIMIT_BYTES,
    )


def _block(dim, want):
    if dim <= want:
        return dim
    b = want
    while dim % b:
        b //= 2
    return b


def _dot(a, b):
    return lax.dot_general(a, b, (((1,), (0,)), ((), ())), preferred_element_type=F32)


def _row_spec(bm, k, buffers=2):
    return pl.BlockSpec((bm, k), lambda i, j: (i, 0), pipeline_mode=pl.Buffered(buffers))


def _col_spec(rows, bn, offset):
    assert offset % bn == 0, (offset, bn)
    return pl.BlockSpec((rows, bn), lambda i, j: (0, offset // bn + j))


def _proj_scale_kernel(x_ref, w_ref, s_ref, o_ref, xb_ref):
    @pl.when(pl.program_id(1) == 0)
    def _():
        xb_ref[...] = x_ref[...].astype(xb_ref.dtype)

    o_ref[...] = (_dot(xb_ref[...], w_ref[...]) * s_ref[...]).astype(o_ref.dtype)


def _proj_scale(x, w, col_scale, name):
    m, k = x.shape
    n = col_scale.shape[1]
    bm, bn = _block(m, TILES[name][0]), _block(n, TILES[name][1])
    return pl.pallas_call(
        _proj_scale_kernel,
        grid=(m // bm, n // bn),
        in_specs=[_row_spec(bm, k), _col_spec(k, bn, 0), _col_spec(1, bn, 0)],
        out_specs=[pl.BlockSpec((bm, bn), lambda i, j: (i, j)), pl.BlockSpec((bm, k), lambda i, j: (i, 0))],
        out_shape=[jax.ShapeDtypeStruct((m, n), BF16), jax.ShapeDtypeStruct((m, k), BF16)],
        compiler_params=_params(2),
        name=name,
    )(x, w, col_scale)


def _proj_residual_kernel(x_ref, w_ref, r_ref, o_ref, *, alpha):
    o_ref[...] = alpha * r_ref[...] + _dot(x_ref[...], w_ref[...])


def _proj_residual(x, w, resid, alpha, name):
    m, k = x.shape
    n = w.shape[1]
    bm, bn = _block(m, TILES[name][0]), _block(n, TILES[name][1])
    return pl.pallas_call(
        functools.partial(_proj_residual_kernel, alpha=alpha),
        grid=(m // bm, n // bn),
        in_specs=[_row_spec(bm, k), _col_spec(k, bn, 0), pl.BlockSpec((bm, bn), lambda i, j: (i, j))],
        out_specs=pl.BlockSpec((bm, bn), lambda i, j: (i, j)),
        out_shape=jax.ShapeDtypeStruct((m, n), F32),
        compiler_params=_params(2),
        name=name,
    )(x, w, resid)


def _normalized(r, mu, rstd, g, b):
    return (r - mu) * rstd * g + b


def _proj_ln_residual_kernel(x_ref, w_ref, r_ref, mu_ref, rstd_ref, g_ref, b_ref, o_ref, *, alpha):
    h = _normalized(r_ref[...], mu_ref[...], rstd_ref[...], g_ref[...], b_ref[...])
    o_ref[...] = alpha * h + _dot(x_ref[...], w_ref[...])


def _proj_ln_residual(x, w, r, mu, rstd, g, b, alpha, name):
    m, k = x.shape
    n = w.shape[1]
    bm, bn = _block(m, TILES[name][0]), _block(n, TILES[name][1])
    tile = pl.BlockSpec((bm, bn), lambda i, j: (i, j))
    stat = pl.BlockSpec((bm, 1), lambda i, j: (i, 0))
    return pl.pallas_call(
        functools.partial(_proj_ln_residual_kernel, alpha=alpha),
        grid=(m // bm, n // bn),
        in_specs=[_row_spec(bm, k), _col_spec(k, bn, 0), tile, stat, stat, _col_spec(1, bn, 0), _col_spec(1, bn, 0)],
        out_specs=tile,
        out_shape=jax.ShapeDtypeStruct((m, n), F32),
        compiler_params=_params(2),
        name=name,
    )(x, w, r, mu, rstd, g, b)


def _conv_body(z, w_ref):
    z1 = pltpu.roll(z, 1, 0)
    z2 = pltpu.roll(z, 2, 0)
    return w_ref[0:1, :] * z2 + w_ref[1:2, :] * z1 + w_ref[2:3, :] * z


def _conv_head(z_head, prev, w_ref):
    row = lax.broadcasted_iota(jnp.int32, z_head.shape, 0)
    z1 = jnp.where(row == 0, prev[SUBLANES - 1:SUBLANES, :], pltpu.roll(z_head, 1, 0))
    z2 = jnp.where(row == 0, prev[SUBLANES - 2:SUBLANES - 1, :],
                   jnp.where(row == 1, prev[SUBLANES - 1:SUBLANES, :], pltpu.roll(z_head, 2, 0)))
    return w_ref[0:1, :] * z2 + w_ref[1:2, :] * z1 + w_ref[2:3, :] * z_head


def _halo(carry_ref, idx, new_tail, seq_start):
    @pl.when(seq_start)
    def _():
        for i in idx[1:]:
            carry_ref[idx[0], i] = jnp.zeros(carry_ref.shape[2:], F32)

    prev = [carry_ref[idx[0], i] for i in idx[1:]]
    for i, t in zip(idx[1:], new_tail):
        carry_ref[idx[0], i] = t
    return prev


def _shortconv_kernel(x_ref, wu_ref, wb_ref, wc_ref, cw_ref, o_ref, carry_ref, *, blocks_per_seq):
    i, j = pl.program_id(0), pl.program_id(1)
    x = x_ref[...]
    u = _dot(x, wu_ref[...])
    g_b = _dot(x, wb_ref[...])
    g_c = _dot(x, wc_ref[...])
    v = g_c * u
    bm = v.shape[0]
    (prev,) = _halo(carry_ref, (j, 0), [v[bm - SUBLANES:, :]], i % blocks_per_seq == 0)
    o_ref[...] = (g_b * _conv_body(v, cw_ref)).astype(o_ref.dtype)
    head = _conv_head(v[:SUBLANES, :], prev, cw_ref)
    o_ref[0:SUBLANES, :] = (g_b[:SUBLANES, :] * head).astype(o_ref.dtype)


def _shortconv(x, w, offsets, conv_w, seq):
    name = "shortconv_proj"
    m, k = x.shape
    n = conv_w.shape[1]
    bm, bn = _block(seq, TILES[name][0]), _block(n, TILES[name][1])
    return pl.pallas_call(
        functools.partial(_shortconv_kernel, blocks_per_seq=seq // bm),
        grid=(m // bm, n // bn),
        in_specs=[_row_spec(bm, k)] + [_col_spec(k, bn, o) for o in offsets] + [_col_spec(CONV_TAPS, bn, 0)],
        out_specs=pl.BlockSpec((bm, bn), lambda i, j: (i, j)),
        out_shape=jax.ShapeDtypeStruct((m, n), BF16),
        scratch_shapes=[pltpu.VMEM((n // bn, 1, SUBLANES, bn), F32)],
        compiler_params=_params(2),
        name=name,
    )(x, w, w, w, conv_w)


def _silu(x):
    return x * jax.nn.sigmoid(x)


def _ffn_up_kernel(h_ref, wg_ref, wv_ref, cg_ref, cv_ref, o_ref, carry_ref, *, blocks_per_seq):
    i, j = pl.program_id(0), pl.program_id(1)
    h = h_ref[...]
    z_g = _dot(h, wg_ref[...].astype(BF16))
    z_v = _dot(h, wv_ref[...].astype(BF16))
    bm = z_g.shape[0]
    prev_g, prev_v = _halo(carry_ref, (j, 0, 1),
                           [z_g[bm - SUBLANES:, :], z_v[bm - SUBLANES:, :]],
                           i % blocks_per_seq == 0)
    o_ref[...] = (_silu(_conv_body(z_g, cg_ref)) * _conv_body(z_v, cv_ref)).astype(o_ref.dtype)
    head_g = _conv_head(z_g[:SUBLANES, :], prev_g, cg_ref)
    head_v = _conv_head(z_v[:SUBLANES, :], prev_v, cv_ref)
    o_ref[0:SUBLANES, :] = (_silu(head_g) * head_v).astype(o_ref.dtype)


def _ffn_up(h, w_up, conv_w, seq):
    name = "ffn_up_conv_act"
    m, k = h.shape
    n = w_up.shape[1] // 2
    bm, bn = _block(seq, TILES[name][0]), _block(n, TILES[name][1])
    return pl.pallas_call(
        functools.partial(_ffn_up_kernel, blocks_per_seq=seq // bm),
        grid=(m // bm, n // bn),
        in_specs=[_row_spec(bm, k, 1), _col_spec(k, bn, 0), _col_spec(k, bn, n),
                  _col_spec(CONV_TAPS, bn, 0), _col_spec(CONV_TAPS, bn, n)],
        out_specs=pl.BlockSpec((bm, bn), lambda i, j: (i, j)),
        out_shape=jax.ShapeDtypeStruct((m, n), BF16),
        scratch_shapes=[pltpu.VMEM((n // bn, 2, SUBLANES, bn), F32)],
        compiler_params=_params(2),
        name=name,
    )(h, w_up, w_up, conv_w, conv_w)


def _attn_kernel(lq1_ref, lk1_ref, lq2_ref, lk2_ref, g_ref, q_ref, k_ref, v_ref, o_ref,
                 m_ref, l_ref, alpha_ref, acc_ref, s_ref, p_ref, *, hd, lam_init):
    qi = pl.program_id(2)
    t = q_ref.shape[0]
    n_lane_tiles = t // LANES

    m_ref[...] = jnp.full(m_ref.shape, MASK_VALUE, F32)
    l_ref[...] = jnp.zeros(l_ref.shape, F32)
    acc_ref[...] = jnp.zeros(acc_ref.shape, F32)

    def scores(kb, slot):
        k = k_ref[pl.ds(pl.multiple_of(kb * t, t), t), :]
        for c in range(2):
            s_ref[slot, c] = lax.dot_general(
                q_ref[:, c * hd:(c + 1) * hd], k[:, c * hd:(c + 1) * hd], (((1,), (1,)), ((), ())),
                preferred_element_type=F32)

    def softmax_rows(slot, c, r, diagonal):
        rows = pl.ds(r, SOFTMAX_ROWS)
        jd = r // LANES
        n_tiles = jd + 1 if diagonal else n_lane_tiles
        tiles = [s_ref[slot, c, rows, j * LANES:(j + 1) * LANES] for j in range(n_tiles)]
        if diagonal:
            tri = (lax.broadcasted_iota(jnp.int32, (SOFTMAX_ROWS, LANES), 1)
                   <= lax.broadcasted_iota(jnp.int32, (SOFTMAX_ROWS, LANES), 0) + r % LANES)
            tiles[jd] = jnp.where(tri, tiles[jd], MASK_VALUE)
        m_old = m_ref[c, rows, :]
        m_new = jnp.maximum(m_old, jnp.max(functools.reduce(jnp.maximum, tiles), axis=-1, keepdims=True))
        alpha = jnp.exp2(m_old - m_new)
        p_tiles = [jnp.exp2(tile - m_new) for tile in tiles]
        l_ref[c, rows, :] = alpha * l_ref[c, rows, :] + functools.reduce(jnp.add, p_tiles)
        p_tiles += [jnp.zeros((SOFTMAX_ROWS, LANES), F32)] * (n_lane_tiles - n_tiles)
        p_ref[slot, c, rows, :] = jnp.concatenate(p_tiles, axis=1).astype(BF16)
        alpha_ref[slot, c, rows, :] = alpha
        m_ref[c, rows, :] = m_new

    def accumulate(kb, slot, diagonal):
        v = v_ref[pl.ds(pl.multiple_of(kb * t, t), t), :]
        for c in range(2):
            for r in range(0, t, SOFTMAX_ROWS):
                softmax_rows(slot, c, r, diagonal)
            alpha = jnp.concatenate([alpha_ref[slot, c]] * (acc_ref.shape[-1] // LANES), axis=1)
            acc_ref[c] = alpha * acc_ref[c] + _dot(p_ref[slot, c], v)

    def pair(i, carry):
        kb = 2 * i
        scores(kb + 1, 1)
        accumulate(kb, 0, False)
        scores(kb + 2, 0)
        accumulate(kb + 1, 1, False)
        return carry

    scores(0, 0)
    lax.fori_loop(0, qi // 2, pair, 0)

    @pl.when(qi % 2 == 0)
    def _():
        accumulate(qi, 0, True)

    @pl.when(qi % 2 == 1)
    def _():
        scores(qi, 1)
        accumulate(qi - 1, 0, False)
        accumulate(qi, 1, True)

    lam = (jnp.exp(jnp.sum(lq1_ref[...] * lk1_ref[...], axis=-1, keepdims=True))
           - jnp.exp(jnp.sum(lq2_ref[...] * lk2_ref[...], axis=-1, keepdims=True))
           + lam_init)
    l1 = jnp.sum(l_ref[0], axis=-1, keepdims=True)
    l2 = jnp.sum(l_ref[1], axis=-1, keepdims=True)
    o = acc_ref[0] / l1 - lam * (acc_ref[1] / l2)
    ms = jnp.mean(o * o, axis=-1, keepdims=True)
    o = o * lax.rsqrt(ms + RMS_EPS) * g_ref[...] * (1.0 - lam_init)
    o_ref[...] = o.astype(o_ref.dtype)


def _diff_attention(qkv, lq1, lk1, lq2, lk2, subln_g, *, batch, seq, heads, hd, lam_init):
    tq = _block(seq, ATTN_BLOCK)
    assert tq % LANES == 0, "query blocks are processed in LANES-row chunks"
    nq = seq // tq
    width = 2 * hd
    vec = pl.BlockSpec((1, hd), lambda b, h, i: (0, 0))
    return pl.pallas_call(
        functools.partial(_attn_kernel, hd=hd, lam_init=lam_init),
        grid=(batch, heads, nq),
        in_specs=[
            vec, vec, vec, vec,
            pl.BlockSpec((1, width), lambda b, h, i: (0, 0)),
            pl.BlockSpec((tq, width), lambda b, h, i: (b * nq + i, h)),
            pl.BlockSpec((seq, width), lambda b, h, i: (b, heads + h)),
            pl.BlockSpec((seq, width), lambda b, h, i: (b, 2 * heads + h)),
        ],
        out_specs=pl.BlockSpec((tq, width), lambda b, h, i: (b * nq + i, h)),
        out_shape=jax.ShapeDtypeStruct((batch * seq, heads * width), BF16),
        scratch_shapes=[
            pltpu.VMEM((2, tq, LANES), F32),
            pltpu.VMEM((2, tq, LANES), F32),
            pltpu.VMEM((2, 2, tq, LANES), F32),
            pltpu.VMEM((2, tq, width), F32),
            pltpu.VMEM((2, 2, tq, tq), F32),
            pltpu.VMEM((2, 2, tq, tq), BF16),
        ],
        compiler_params=_params(3),
        name="diff_attention",
    )(lq1, lk1, lq2, lk2, subln_g, qkv, qkv, qkv)


def _merge_kernel(x_ref, a_ref, c_ref, wga_ref, wgc_ref, wa_ref, wc_ref, o_ref):
    x = x_ref[...]
    gate_a = jax.nn.sigmoid(_dot(x, wga_ref[...]))
    gate_c = jax.nn.sigmoid(_dot(x, wgc_ref[...]))
    y_a = _dot(a_ref[...], wa_ref[...])
    y_c = _dot(c_ref[...], wc_ref[...])
    o_ref[...] = (gate_a * y_a + gate_c * y_c).astype(o_ref.dtype)


def _merge(x, attn, yc, w, gate_offsets, wa, wc):
    name = "gated_merge"
    m, k = x.shape
    ka, kc = attn.shape[1], yc.shape[1]
    n = wa.shape[1]
    bm, bn = _block(m, TILES[name][0]), _block(n, TILES[name][1])
    return pl.pallas_call(
        _merge_kernel,
        grid=(m // bm, n // bn),
        in_specs=[
            _row_spec(bm, k), _row_spec(bm, ka), _row_spec(bm, kc),
            _col_spec(k, bn, gate_offsets[0]), _col_spec(k, bn, gate_offsets[1]),
            _col_spec(ka, bn, 0), _col_spec(kc, bn, 0),
        ],
        out_specs=pl.BlockSpec((bm, bn), lambda i, j: (i, j)),
        out_shape=jax.ShapeDtypeStruct((m, n), BF16),
        compiler_params=_params(2),
        name=name,
    )(x, attn, yc, w, w, wa, wc)


def _row_stats(r):
    mu = jnp.mean(r, axis=-1, keepdims=True)
    rc = r - mu
    var = jnp.mean(rc * rc, axis=-1, keepdims=True)
    return mu, lax.rsqrt(var + LN_EPS)


def _layer_norm_kernel(r_ref, g_ref, b_ref, o_ref):
    r = r_ref[...]
    mu, rstd = _row_stats(r)
    o_ref[...] = _normalized(r, mu, rstd, g_ref[...], b_ref[...]).astype(o_ref.dtype)


def _layer_norm_stats_kernel(r_ref, g_ref, b_ref, o_ref, mu_ref, rstd_ref):
    r = r_ref[...]
    mu, rstd = _row_stats(r)
    o_ref[...] = _normalized(r, mu, rstd, g_ref[...], b_ref[...]).astype(o_ref.dtype)
    mu_ref[...] = mu
    rstd_ref[...] = rstd


def _layer_norm(r, g, b, out_dtype, name, with_stats=False):
    m, d = r.shape
    bm = _block(m, LN_ROWS)
    row = pl.BlockSpec((bm, d), lambda i: (i, 0))
    vec = pl.BlockSpec((1, d), lambda i: (0, 0))
    stat = pl.BlockSpec((bm, 1), lambda i: (i, 0))
    out = jax.ShapeDtypeStruct((m, d), out_dtype)
    stat_shape = jax.ShapeDtypeStruct((m, 1), F32)
    return pl.pallas_call(
        _layer_norm_stats_kernel if with_stats else _layer_norm_kernel,
        grid=(m // bm,),
        in_specs=[row, vec, vec],
        out_specs=[row, stat, stat] if with_stats else row,
        out_shape=[out, stat_shape, stat_shape] if with_stats else out,
        compiler_params=_params(1),
        name=name,
    )(r, g, b)


def _lambda_init(layer_idx):
    return 0.8 - 0.6 * math.exp(-0.3 * layer_idx)


def kernel(x, w_in, lambda_q1, lambda_k1, lambda_q2, lambda_k2, subln_g, conv_mix_w, w_attn_out,
           w_conv_out, w_o, ln1_g, ln1_b, w_up, ffn_conv_w, w_down, ln2_g, ln2_b):
    batch, seq, d_model = x.shape
    depth = w_in.shape[0]
    hd = lambda_q1.shape[-1]
    v_width = w_attn_out.shape[1]
    heads = v_width // (2 * hd)
    qk_width = heads * 2 * hd
    conv_width = conv_mix_w.shape[-1]
    alpha = (2.0 * depth) ** 0.25
    rows = batch * seq

    o_u = 2 * qk_width + v_width
    o_b = o_u + conv_width
    o_c = o_b + conv_width
    o_ga = o_c + conv_width
    o_gc = o_ga + d_model

    q_scale = LOG2E / math.sqrt(hd)
    col_scale = jnp.concatenate(
        [jnp.full((1, qk_width), q_scale, F32), jnp.ones((1, o_u - qk_width), F32)], axis=1)

    h = x.reshape(rows, d_model)
    for l in range(depth):
        lam_init = _lambda_init(l)
        w_in_bf = w_in[l].astype(BF16)
        g1, b1 = ln1_g[l][None], ln1_b[l][None]

        qkv, h_bf = _proj_scale(h, w_in_bf, col_scale, "qkv_proj")
        attn = _diff_attention(
            qkv, lambda_q1[l][None], lambda_k1[l][None], lambda_q2[l][None], lambda_k2[l][None],
            subln_g[l][None], batch=batch, seq=seq, heads=heads, hd=hd, lam_init=lam_init)
        yc = _shortconv(h_bf, w_in_bf, (o_u, o_b, o_c), conv_mix_w[l], seq)
        merged = _merge(h_bf, attn, yc, w_in_bf, (o_ga, o_gc),
                        w_attn_out[l].astype(BF16), w_conv_out[l].astype(BF16))

        r1 = _proj_residual(merged, w_o[l].astype(BF16), h, alpha, "wo_proj")
        h1_bf, mu1, rstd1 = _layer_norm(r1, g1, b1, BF16, "ln1", with_stats=True)

        act = _ffn_up(h1_bf, w_up[l], ffn_conv_w[l], seq)
        r2 = _proj_ln_residual(act, w_down[l].astype(BF16), r1, mu1, rstd1, g1, b1, alpha, "down_proj")
        h = _layer_norm(r2, ln2_g[l][None], ln2_b[l][None], F32, "ln2")
    return h.reshape(batch, seq, d_model)
```

```python
import functools
import math

import jax
import jax.numpy as jnp
from jax import lax
from jax.experimental import pallas as pl
from jax.experimental.pallas import tpu as pltpu

LN_EPS = 1e-5
RMS_EPS = 1e-5
LOG2E = 1.4426950408889634
CONV_TAPS = 3
SUBLANES = 8
LANES = 128
MASK_VALUE = -1e30
VMEM_LIMIT_BYTES = 56 * 1024 * 1024

TILES = {
    "qkv_proj": (1024, 1024),
    "shortconv_proj": (1024, 256),
    "gated_merge": (512, 512),
    "wo_proj": (1024, 512),
    "ffn_up_conv_act": (2048, 256),
    "down_proj": (512, 512),
}
ATTN_BLOCK = 512
SOFTMAX_ROWS = 64
LN_ROWS = 256

F32 = jnp.float32
BF16 = jnp.bfloat16


def _params(n_axes):
    return pltpu.CompilerParams(
        dimension_semantics=("arbitrary",) * n_axes,
        vmem_limit_bytes=VMEM_LIMIT_BYTES,
    )


def _block(dim, want):
    if dim <= want:
        return dim
    b = want
    while dim % b:
        b //= 2
    return b


def _dot(a, b):
    return lax.dot_general(a, b, (((1,), (0,)), ((), ())), preferred_element_type=F32)


def _row_spec(bm, k, buffers=2):
    return pl.BlockSpec((bm, k), lambda i, j: (i, 0), pipeline_mode=pl.Buffered(buffers))


def _whole_spec(a):
    return pl.BlockSpec(a.shape, lambda i, j: (0, 0))


def _col_spec(rows, bn, offset):
    assert offset % bn == 0, (offset, bn)
    return pl.BlockSpec((rows, bn), lambda i, j: (0, offset // bn + j))


def _cols(ref, bn, offset=0):
    start = pl.multiple_of(offset + pl.program_id(1) * bn, LANES)
    return ref[:, pl.ds(start, bn)]


def _proj_scale_kernel(x_ref, w_ref, s_ref, o_ref):
    o_ref[...] = (_dot(x_ref[...], w_ref[...]) * _cols(s_ref, o_ref.shape[1])).astype(o_ref.dtype)


def _proj_scale(x, w, col_scale, name):
    m, k = x.shape
    n = col_scale.shape[1]
    bm, bn = _block(m, TILES[name][0]), _block(n, TILES[name][1])
    return pl.pallas_call(
        _proj_scale_kernel,
        grid=(m // bm, n // bn),
        in_specs=[_row_spec(bm, k), _col_spec(k, bn, 0), _whole_spec(col_scale)],
        out_specs=pl.BlockSpec((bm, bn), lambda i, j: (i, j)),
        out_shape=jax.ShapeDtypeStruct((m, n), BF16),
        compiler_params=_params(2),
        name=name,
    )(x, w, col_scale)


def _proj_residual_kernel(x_ref, w_ref, r_ref, o_ref, *, alpha):
    o_ref[...] = alpha * r_ref[...] + _dot(x_ref[...], w_ref[...])


def _proj_residual(x, w, resid, alpha, name):
    m, k = x.shape
    n = w.shape[1]
    bm, bn = _block(m, TILES[name][0]), _block(n, TILES[name][1])
    return pl.pallas_call(
        functools.partial(_proj_residual_kernel, alpha=alpha),
        grid=(m // bm, n // bn),
        in_specs=[_row_spec(bm, k), _col_spec(k, bn, 0), pl.BlockSpec((bm, bn), lambda i, j: (i, j))],
        out_specs=pl.BlockSpec((bm, bn), lambda i, j: (i, j)),
        out_shape=jax.ShapeDtypeStruct((m, n), F32),
        compiler_params=_params(2),
        name=name,
    )(x, w, resid)


def _normalized(r, mu, rstd, g, b):
    return (r - mu) * rstd * g + b


def _proj_ln_residual_kernel(x_ref, w_ref, r_ref, mu_ref, rstd_ref, g_ref, b_ref, o_ref, *, alpha):
    bn = o_ref.shape[1]
    h = _normalized(r_ref[...], mu_ref[...], rstd_ref[...], _cols(g_ref, bn), _cols(b_ref, bn))
    o_ref[...] = alpha * h + _dot(x_ref[...], w_ref[...])


def _proj_ln_residual(x, w, r, mu, rstd, g, b, alpha, name):
    m, k = x.shape
    n = w.shape[1]
    bm, bn = _block(m, TILES[name][0]), _block(n, TILES[name][1])
    tile = pl.BlockSpec((bm, bn), lambda i, j: (i, j))
    stat = pl.BlockSpec((bm, 1), lambda i, j: (i, 0))
    return pl.pallas_call(
        functools.partial(_proj_ln_residual_kernel, alpha=alpha),
        grid=(m // bm, n // bn),
        in_specs=[_row_spec(bm, k), _col_spec(k, bn, 0), tile, stat, stat, _whole_spec(g), _whole_spec(b)],
        out_specs=tile,
        out_shape=jax.ShapeDtypeStruct((m, n), F32),
        compiler_params=_params(2),
        name=name,
    )(x, w, r, mu, rstd, g, b)


def _conv_body(z, taps):
    z1 = pltpu.roll(z, 1, 0)
    z2 = pltpu.roll(z, 2, 0)
    return taps[0:1, :] * z2 + taps[1:2, :] * z1 + taps[2:3, :] * z


def _conv_head(z_head, prev, taps):
    row = lax.broadcasted_iota(jnp.int32, z_head.shape, 0)
    z1 = jnp.where(row == 0, prev[SUBLANES - 1:SUBLANES, :], pltpu.roll(z_head, 1, 0))
    z2 = jnp.where(row == 0, prev[SUBLANES - 2:SUBLANES - 1, :],
                   jnp.where(row == 1, prev[SUBLANES - 1:SUBLANES, :], pltpu.roll(z_head, 2, 0)))
    return taps[0:1, :] * z2 + taps[1:2, :] * z1 + taps[2:3, :] * z_head


def _halo(carry_ref, idx, new_tail, seq_start):
    @pl.when(seq_start)
    def _():
        for i in idx[1:]:
            carry_ref[idx[0], i] = jnp.zeros(carry_ref.shape[2:], F32)

    prev = [carry_ref[idx[0], i] for i in idx[1:]]
    for i, t in zip(idx[1:], new_tail):
        carry_ref[idx[0], i] = t
    return prev


def _shortconv_kernel(x_ref, wu_ref, wb_ref, wc_ref, cw_ref, o_ref, carry_ref, *, blocks_per_seq):
    i, j = pl.program_id(0), pl.program_id(1)
    x = x_ref[...]
    u = _dot(x, wu_ref[...])
    g_b = _dot(x, wb_ref[...])
    g_c = _dot(x, wc_ref[...])
    v = g_c * u
    bm = v.shape[0]
    (prev,) = _halo(carry_ref, (j, 0), [v[bm - SUBLANES:, :]], i % blocks_per_seq == 0)
    taps = _cols(cw_ref, v.shape[1])
    o_ref[...] = (g_b * _conv_body(v, taps)).astype(o_ref.dtype)
    head = _conv_head(v[:SUBLANES, :], prev, taps)
    o_ref[0:SUBLANES, :] = (g_b[:SUBLANES, :] * head).astype(o_ref.dtype)


def _shortconv(x, w, offsets, conv_w, seq):
    name = "shortconv_proj"
    m, k = x.shape
    n = conv_w.shape[1]
    bm, bn = _block(seq, TILES[name][0]), _block(n, TILES[name][1])
    return pl.pallas_call(
        functools.partial(_shortconv_kernel, blocks_per_seq=seq // bm),
        grid=(m // bm, n // bn),
        in_specs=[_row_spec(bm, k)] + [_col_spec(k, bn, o) for o in offsets] + [_whole_spec(conv_w)],
        out_specs=pl.BlockSpec((bm, bn), lambda i, j: (i, j)),
        out_shape=jax.ShapeDtypeStruct((m, n), BF16),
        scratch_shapes=[pltpu.VMEM((n // bn, 1, SUBLANES, bn), F32)],
        compiler_params=_params(2),
        name=name,
    )(x, w, w, w, conv_w)


def _silu(x):
    return x * jax.nn.sigmoid(x)


def _ffn_up_kernel(h_ref, wg_ref, wv_ref, cw_ref, o_ref, carry_ref, *, blocks_per_seq):
    i, j = pl.program_id(0), pl.program_id(1)
    h = h_ref[...]
    z_g = _dot(h, wg_ref[...].astype(BF16))
    z_v = _dot(h, wv_ref[...].astype(BF16))
    bm, bn = z_g.shape
    prev_g, prev_v = _halo(carry_ref, (j, 0, 1),
                           [z_g[bm - SUBLANES:, :], z_v[bm - SUBLANES:, :]],
                           i % blocks_per_seq == 0)
    taps_g = _cols(cw_ref, bn)
    taps_v = _cols(cw_ref, bn, cw_ref.shape[1] // 2)
    o_ref[...] = (_silu(_conv_body(z_g, taps_g)) * _conv_body(z_v, taps_v)).astype(o_ref.dtype)
    head_g = _conv_head(z_g[:SUBLANES, :], prev_g, taps_g)
    head_v = _conv_head(z_v[:SUBLANES, :], prev_v, taps_v)
    o_ref[0:SUBLANES, :] = (_silu(head_g) * head_v).astype(o_ref.dtype)


def _ffn_up(h, w_up, conv_w, seq):
    name = "ffn_up_conv_act"
    m, k = h.shape
    n = w_up.shape[1] // 2
    bm, bn = _block(seq, TILES[name][0]), _block(n, TILES[name][1])
    return pl.pallas_call(
        functools.partial(_ffn_up_kernel, blocks_per_seq=seq // bm),
        grid=(m // bm, n // bn),
        in_specs=[_row_spec(bm, k, 1), _col_spec(k, bn, 0), _col_spec(k, bn, n), _whole_spec(conv_w)],
        out_specs=pl.BlockSpec((bm, bn), lambda i, j: (i, j)),
        out_shape=jax.ShapeDtypeStruct((m, n), BF16),
        scratch_shapes=[pltpu.VMEM((n // bn, 2, SUBLANES, bn), F32)],
        compiler_params=_params(2),
        name=name,
    )(h, w_up, w_up, conv_w)


def _attn_kernel(lq1_ref, lk1_ref, lq2_ref, lk2_ref, g_ref, q_ref, k_ref, v_ref, o_ref,
                 m_ref, l_ref, alpha_ref, acc_ref, s_ref, p_ref, *, hd, lam_init):
    qi = pl.program_id(2)
    t = q_ref.shape[0]
    n_lane_tiles = t // LANES

    m_ref[...] = jnp.full(m_ref.shape, MASK_VALUE, F32)
    l_ref[...] = jnp.zeros(l_ref.shape, F32)
    acc_ref[...] = jnp.zeros(acc_ref.shape, F32)

    def scores(kb, slot):
        k = k_ref[pl.ds(pl.multiple_of(kb * t, t), t), :]
        for c in range(2):
            s_ref[slot, c] = lax.dot_general(
                q_ref[:, c * hd:(c + 1) * hd], k[:, c * hd:(c + 1) * hd], (((1,), (1,)), ((), ())),
                preferred_element_type=F32)

    def softmax_rows(slot, c, r, diagonal):
        rows = pl.ds(r, SOFTMAX_ROWS)
        jd = r // LANES
        n_tiles = jd + 1 if diagonal else n_lane_tiles
        tiles = [s_ref[slot, c, rows, j * LANES:(j + 1) * LANES] for j in range(n_tiles)]
        if diagonal:
            tri = (lax.broadcasted_iota(jnp.int32, (SOFTMAX_ROWS, LANES), 1)
                   <= lax.broadcasted_iota(jnp.int32, (SOFTMAX_ROWS, LANES), 0) + r % LANES)
            tiles[jd] = jnp.where(tri, tiles[jd], MASK_VALUE)
        m_old = m_ref[c, rows, :]
        m_new = jnp.maximum(m_old, jnp.max(functools.reduce(jnp.maximum, tiles), axis=-1, keepdims=True))
        alpha = jnp.exp2(m_old - m_new)
        p_tiles = [jnp.exp2(tile - m_new) for tile in tiles]
        l_ref[c, rows, :] = alpha * l_ref[c, rows, :] + functools.reduce(jnp.add, p_tiles)
        p_tiles += [jnp.zeros((SOFTMAX_ROWS, LANES), F32)] * (n_lane_tiles - n_tiles)
        p_ref[slot, c, rows, :] = jnp.concatenate(p_tiles, axis=1).astype(BF16)
        alpha_ref[slot, c, rows, :] = alpha
        m_ref[c, rows, :] = m_new

    def accumulate(kb, slot, diagonal):
        v = v_ref[pl.ds(pl.multiple_of(kb * t, t), t), :]
        for c in range(2):
            for r in range(0, t, SOFTMAX_ROWS):
                softmax_rows(slot, c, r, diagonal)
            alpha = jnp.concatenate([alpha_ref[slot, c]] * (acc_ref.shape[-1] // LANES), axis=1)
            acc_ref[c] = alpha * acc_ref[c] + _dot(p_ref[slot, c], v)

    def pair(i, carry):
        kb = 2 * i
        scores(kb + 1, 1)
        accumulate(kb, 0, False)
        scores(kb + 2, 0)
        accumulate(kb + 1, 1, False)
        return carry

    scores(0, 0)
    lax.fori_loop(0, qi // 2, pair, 0)

    @pl.when(qi % 2 == 0)
    def _():
        accumulate(qi, 0, True)

    @pl.when(qi % 2 == 1)
    def _():
        scores(qi, 1)
        accumulate(qi - 1, 0, False)
        accumulate(qi, 1, True)

    lam = (jnp.exp(jnp.sum(lq1_ref[...] * lk1_ref[...], axis=-1, keepdims=True))
           - jnp.exp(jnp.sum(lq2_ref[...] * lk2_ref[...], axis=-1, keepdims=True))
           + lam_init)
    l1 = jnp.sum(l_ref[0], axis=-1, keepdims=True)
    l2 = jnp.sum(l_ref[1], axis=-1, keepdims=True)
    o = acc_ref[0] / l1 - lam * (acc_ref[1] / l2)
    ms = jnp.mean(o * o, axis=-1, keepdims=True)
    o = o * lax.rsqrt(ms + RMS_EPS) * g_ref[...] * (1.0 - lam_init)
    o_ref[...] = o.astype(o_ref.dtype)


def _diff_attention(qkv, lq1, lk1, lq2, lk2, subln_g, *, batch, seq, heads, hd, lam_init):
    tq = _block(seq, ATTN_BLOCK)
    assert tq % LANES == 0, "query blocks are processed in LANES-row chunks"
    nq = seq // tq
    width = 2 * hd
    vec = pl.BlockSpec((1, hd), lambda b, h, i: (0, 0))
    return pl.pallas_call(
        functools.partial(_attn_kernel, hd=hd, lam_init=lam_init),
        grid=(batch, heads, nq),
        in_specs=[
            vec, vec, vec, vec,
            pl.BlockSpec((1, width), lambda b, h, i: (0, 0)),
            pl.BlockSpec((tq, width), lambda b, h, i: (b * nq + i, h)),
            pl.BlockSpec((seq, width), lambda b, h, i: (b, heads + h)),
            pl.BlockSpec((seq, width), lambda b, h, i: (b, 2 * heads + h)),
        ],
        out_specs=pl.BlockSpec((tq, width), lambda b, h, i: (b * nq + i, h)),
        out_shape=jax.ShapeDtypeStruct((batch * seq, heads * width), BF16),
        scratch_shapes=[
            pltpu.VMEM((2, tq, LANES), F32),
            pltpu.VMEM((2, tq, LANES), F32),
            pltpu.VMEM((2, 2, tq, LANES), F32),
            pltpu.VMEM((2, tq, width), F32),
            pltpu.VMEM((2, 2, tq, tq), F32),
            pltpu.VMEM((2, 2, tq, tq), BF16),
        ],
        compiler_params=_params(3),
        name="diff_attention",
    )(lq1, lk1, lq2, lk2, subln_g, qkv, qkv, qkv)


def _merge_kernel(x_ref, a_ref, c_ref, wga_ref, wgc_ref, wa_ref, wc_ref, o_ref):
    x = x_ref[...]
    gate_a = jax.nn.sigmoid(_dot(x, wga_ref[...]))
    gate_c = jax.nn.sigmoid(_dot(x, wgc_ref[...]))
    y_a = _dot(a_ref[...], wa_ref[...])
    y_c = _dot(c_ref[...], wc_ref[...])
    o_ref[...] = (gate_a * y_a + gate_c * y_c).astype(o_ref.dtype)


def _merge(x, attn, yc, w, gate_offsets, wa, wc):
    name = "gated_merge"
    m, k = x.shape
    ka, kc = attn.shape[1], yc.shape[1]
    n = wa.shape[1]
    bm, bn = _block(m, TILES[name][0]), _block(n, TILES[name][1])
    return pl.pallas_call(
        _merge_kernel,
        grid=(m // bm, n // bn),
        in_specs=[
            _row_spec(bm, k), _row_spec(bm, ka), _row_spec(bm, kc),
            _col_spec(k, bn, gate_offsets[0]), _col_spec(k, bn, gate_offsets[1]),
            _col_spec(ka, bn, 0), _col_spec(kc, bn, 0),
        ],
        out_specs=pl.BlockSpec((bm, bn), lambda i, j: (i, j)),
        out_shape=jax.ShapeDtypeStruct((m, n), BF16),
        compiler_params=_params(2),
        name=name,
    )(x, attn, yc, w, w, wa, wc)


def _row_stats(r):
    mu = jnp.mean(r, axis=-1, keepdims=True)
    rc = r - mu
    var = jnp.mean(rc * rc, axis=-1, keepdims=True)
    return mu, lax.rsqrt(var + LN_EPS)


def _layer_norm_kernel(r_ref, g_ref, b_ref, o_ref):
    r = r_ref[...]
    mu, rstd = _row_stats(r)
    o_ref[...] = _normalized(r, mu, rstd, g_ref[...], b_ref[...]).astype(o_ref.dtype)


def _layer_norm_stats_kernel(r_ref, g_ref, b_ref, o_ref, mu_ref, rstd_ref):
    r = r_ref[...]
    mu, rstd = _row_stats(r)
    o_ref[...] = _normalized(r, mu, rstd, g_ref[...], b_ref[...]).astype(o_ref.dtype)
    mu_ref[...] = mu
    rstd_ref[...] = rstd


def _layer_norm(r, g, b, out_dtype, name, with_stats=False):
    m, d = r.shape
    bm = _block(m, LN_ROWS)
    row = pl.BlockSpec((bm, d), lambda i: (i, 0))
    vec = pl.BlockSpec((1, d), lambda i: (0, 0))
    stat = pl.BlockSpec((bm, 1), lambda i: (i, 0))
    out = jax.ShapeDtypeStruct((m, d), out_dtype)
    stat_shape = jax.ShapeDtypeStruct((m, 1), F32)
    return pl.pallas_call(
        _layer_norm_stats_kernel if with_stats else _layer_norm_kernel,
        grid=(m // bm,),
        in_specs=[row, vec, vec],
        out_specs=[row, stat, stat] if with_stats else row,
        out_shape=[out, stat_shape, stat_shape] if with_stats else out,
        compiler_params=_params(1),
        name=name,
    )(r, g, b)


def _lambda_init(layer_idx):
    return 0.8 - 0.6 * math.exp(-0.3 * layer_idx)


def kernel(x, w_in, lambda_q1, lambda_k1, lambda_q2, lambda_k2, subln_g, conv_mix_w, w_attn_out,
           w_conv_out, w_o, ln1_g, ln1_b, w_up, ffn_conv_w, w_down, ln2_g, ln2_b):
    batch, seq, d_model = x.shape
    depth = w_in.shape[0]
    hd = lambda_q1.shape[-1]
    v_width = w_attn_out.shape[1]
    heads = v_width // (2 * hd)
    qk_width = heads * 2 * hd
    conv_width = conv_mix_w.shape[-1]
    alpha = (2.0 * depth) ** 0.25
    rows = batch * seq

    o_u = 2 * qk_width + v_width
    o_b = o_u + conv_width
    o_c = o_b + conv_width
    o_ga = o_c + conv_width
    o_gc = o_ga + d_model

    q_scale = LOG2E / math.sqrt(hd)
    col_scale = jnp.concatenate(
        [jnp.full((1, qk_width), q_scale, F32), jnp.ones((1, o_u - qk_width), F32)], axis=1)

    h = x.reshape(rows, d_model)
    for l in range(depth):
        lam_init = _lambda_init(l)
        h_bf = h.astype(BF16)
        w_in_bf = w_in[l].astype(BF16)
        g1, b1 = ln1_g[l][None], ln1_b[l][None]

        qkv = _proj_scale(h_bf, w_in_bf, col_scale, "qkv_proj")
        attn = _diff_attention(
            qkv, lambda_q1[l][None], lambda_k1[l][None], lambda_q2[l][None], lambda_k2[l][None],
            subln_g[l][None], batch=batch, seq=seq, heads=heads, hd=hd, lam_init=lam_init)
        yc = _shortconv(h_bf, w_in_bf, (o_u, o_b, o_c), conv_mix_w[l], seq)
        merged = _merge(h_bf, attn, yc, w_in_bf, (o_ga, o_gc),
                        w_attn_out[l].astype(BF16), w_conv_out[l].astype(BF16))

        r1 = _proj_residual(merged, w_o[l].astype(BF16), h, alpha, "wo_proj")
        h1_bf, mu1, rstd1 = _layer_norm(r1, g1, b1, BF16, "ln1", with_stats=True)

        act = _ffn_up(h1_bf, w_up[l], ffn_conv_w[l], seq)
        r2 = _proj_ln_residual(act, w_down[l].astype(BF16), r1, mu1, rstd1, g1, b1, alpha, "down_proj")
        h = _layer_norm(r2, ln2_g[l][None], ln2_b[l][None], F32, "ln2")
    return h.reshape(batch, seq, d_model)
```

```python
import functools
import math

import jax
import jax.numpy as jnp
from jax import lax
from jax.experimental import pallas as pl
from jax.experimental.pallas import tpu as pltpu

LN_EPS = 1e-5
RMS_EPS = 1e-5
LOG2E = 1.4426950408889634
CONV_TAPS = 3
SUBLANES = 8
LANES = 128
MASK_VALUE = -1e30
VMEM_LIMIT_BYTES = 56 * 1024 * 1024

TILES = {
    "qkv_proj": (1024, 1024),
    "shortconv_proj": (1024, 256),
    "gated_merge": (512, 512),
    "wo_proj": (1024, 512),
    "ffn_up_conv_act": (2048, 256),
    "down_proj": (512, 512),
}
ATTN_BLOCK = 512
SOFTMAX_ROWS = 64
LN_ROWS = 256

F32 = jnp.float32
BF16 = jnp.bfloat16


def _params(n_axes):
    return pltpu.CompilerParams(
        dimension_semantics=("arbitrary",) * n_axes,
        vmem_limit_bytes=VMEM_LIMIT_BYTES,
    )


def _block(dim, want):
    if dim <= want:
        return dim
    b = want
    while dim % b:
        b //= 2
    return b


def _dot(a, b):
    return lax.dot_general(a, b, (((1,), (0,)), ((), ())), preferred_element_type=F32)


def _row_spec(bm, k, buffers=2):
    return pl.BlockSpec((bm, k), lambda i, j: (i, 0), pipeline_mode=pl.Buffered(buffers))


def _whole_spec(a):
    return pl.BlockSpec(a.shape, lambda i, j: (0, 0))


def _col_spec(rows, bn, offset):
    assert offset % bn == 0, (offset, bn)
    return pl.BlockSpec((rows, bn), lambda i, j: (0, offset // bn + j))


def _cols(ref, bn, offset=0):
    start = pl.multiple_of(offset + pl.program_id(1) * bn, LANES)
    return ref[:, pl.ds(start, bn)]


def _proj_scale_kernel(x_ref, w_ref, s_ref, o_ref):
    o_ref[...] = (_dot(x_ref[...], w_ref[...]) * _cols(s_ref, o_ref.shape[1])).astype(o_ref.dtype)


def _proj_scale(x, w, col_scale, name):
    m, k = x.shape
    n = col_scale.shape[1]
    bm, bn = _block(m, TILES[name][0]), _block(n, TILES[name][1])
    return pl.pallas_call(
        _proj_scale_kernel,
        grid=(m // bm, n // bn),
        in_specs=[_row_spec(bm, k), _col_spec(k, bn, 0), _whole_spec(col_scale)],
        out_specs=pl.BlockSpec((bm, bn), lambda i, j: (i, j)),
        out_shape=jax.ShapeDtypeStruct((m, n), BF16),
        compiler_params=_params(2),
        name=name,
    )(x, w, col_scale)


def _proj_residual_kernel(x_ref, w_ref, r_ref, o_ref, *, alpha):
    o_ref[...] = alpha * r_ref[...] + _dot(x_ref[...], w_ref[...])


def _proj_residual(x, w, resid, alpha, name):
    m, k = x.shape
    n = w.shape[1]
    bm, bn = _block(m, TILES[name][0]), _block(n, TILES[name][1])
    return pl.pallas_call(
        functools.partial(_proj_residual_kernel, alpha=alpha),
        grid=(m // bm, n // bn),
        in_specs=[_row_spec(bm, k), _col_spec(k, bn, 0), pl.BlockSpec((bm, bn), lambda i, j: (i, j))],
        out_specs=pl.BlockSpec((bm, bn), lambda i, j: (i, j)),
        out_shape=jax.ShapeDtypeStruct((m, n), F32),
        compiler_params=_params(2),
        name=name,
    )(x, w, resid)


def _normalized(r, mu, rstd, g, b):
    return (r - mu) * rstd * g + b


def _proj_ln_residual_kernel(x_ref, w_ref, r_ref, mu_ref, rstd_ref, g_ref, b_ref, o_ref, *, alpha):
    bn = o_ref.shape[1]
    h = _normalized(r_ref[...], mu_ref[...], rstd_ref[...], _cols(g_ref, bn), _cols(b_ref, bn))
    o_ref[...] = alpha * h + _dot(x_ref[...], w_ref[...])


def _proj_ln_residual(x, w, r, mu, rstd, g, b, alpha, name):
    m, k = x.shape
    n = w.shape[1]
    bm, bn = _block(m, TILES[name][0]), _block(n, TILES[name][1])
    tile = pl.BlockSpec((bm, bn), lambda i, j: (i, j))
    stat = pl.BlockSpec((bm, 1), lambda i, j: (i, 0))
    return pl.pallas_call(
        functools.partial(_proj_ln_residual_kernel, alpha=alpha),
        grid=(m // bm, n // bn),
        in_specs=[_row_spec(bm, k), _col_spec(k, bn, 0), tile, stat, stat, _whole_spec(g), _whole_spec(b)],
        out_specs=tile,
        out_shape=jax.ShapeDtypeStruct((m, n), F32),
        compiler_params=_params(2),
        name=name,
    )(x, w, r, mu, rstd, g, b)


def _conv_body(z, taps):
    z1 = pltpu.roll(z, 1, 0)
    z2 = pltpu.roll(z, 2, 0)
    return taps[0:1, :] * z2 + taps[1:2, :] * z1 + taps[2:3, :] * z


def _conv_head(z_head, prev, taps):
    row = lax.broadcasted_iota(jnp.int32, z_head.shape, 0)
    z1 = jnp.where(row == 0, prev[SUBLANES - 1:SUBLANES, :], pltpu.roll(z_head, 1, 0))
    z2 = jnp.where(row == 0, prev[SUBLANES - 2:SUBLANES - 1, :],
                   jnp.where(row == 1, prev[SUBLANES - 1:SUBLANES, :], pltpu.roll(z_head, 2, 0)))
    return taps[0:1, :] * z2 + taps[1:2, :] * z1 + taps[2:3, :] * z_head


def _halo(carry_ref, idx, new_tail, seq_start):
    @pl.when(seq_start)
    def _():
        for i in idx[1:]:
            carry_ref[idx[0], i] = jnp.zeros(carry_ref.shape[2:], F32)

    prev = [carry_ref[idx[0], i] for i in idx[1:]]
    for i, t in zip(idx[1:], new_tail):
        carry_ref[idx[0], i] = t
    return prev


def _shortconv_kernel(x_ref, wu_ref, wb_ref, wc_ref, cw_ref, o_ref, carry_ref, *, blocks_per_seq):
    i, j = pl.program_id(0), pl.program_id(1)
    x = x_ref[...]
    u = _dot(x, wu_ref[...])
    g_b = _dot(x, wb_ref[...])
    g_c = _dot(x, wc_ref[...])
    v = g_c * u
    bm = v.shape[0]
    (prev,) = _halo(carry_ref, (j, 0), [v[bm - SUBLANES:, :]], i % blocks_per_seq == 0)
    taps = _cols(cw_ref, v.shape[1])
    o_ref[...] = (g_b * _conv_body(v, taps)).astype(o_ref.dtype)
    head = _conv_head(v[:SUBLANES, :], prev, taps)
    o_ref[0:SUBLANES, :] = (g_b[:SUBLANES, :] * head).astype(o_ref.dtype)


def _shortconv(x, w, offsets, conv_w, seq):
    name = "shortconv_proj"
    m, k = x.shape
    n = conv_w.shape[1]
    bm, bn = _block(seq, TILES[name][0]), _block(n, TILES[name][1])
    return pl.pallas_call(
        functools.partial(_shortconv_kernel, blocks_per_seq=seq // bm),
        grid=(m // bm, n // bn),
        in_specs=[_row_spec(bm, k)] + [_col_spec(k, bn, o) for o in offsets] + [_whole_spec(conv_w)],
        out_specs=pl.BlockSpec((bm, bn), lambda i, j: (i, j)),
        out_shape=jax.ShapeDtypeStruct((m, n), BF16),
        scratch_shapes=[pltpu.VMEM((n // bn, 1, SUBLANES, bn), F32)],
        compiler_params=_params(2),
        name=name,
    )(x, w, w, w, conv_w)


def _silu(x):
    return x * jax.nn.sigmoid(x)


def _ffn_up_kernel(h_ref, wg_ref, wv_ref, cw_ref, o_ref, carry_ref, *, blocks_per_seq):
    i, j = pl.program_id(0), pl.program_id(1)
    h = h_ref[...]
    z_g = _dot(h, wg_ref[...].astype(BF16))
    z_v = _dot(h, wv_ref[...].astype(BF16))
    bm, bn = z_g.shape
    prev_g, prev_v = _halo(carry_ref, (j, 0, 1),
                           [z_g[bm - SUBLANES:, :], z_v[bm - SUBLANES:, :]],
                           i % blocks_per_seq == 0)
    taps_g = _cols(cw_ref, bn)
    taps_v = _cols(cw_ref, bn, cw_ref.shape[1] // 2)
    o_ref[...] = (_silu(_conv_body(z_g, taps_g)) * _conv_body(z_v, taps_v)).astype(o_ref.dtype)
    head_g = _conv_head(z_g[:SUBLANES, :], prev_g, taps_g)
    head_v = _conv_head(z_v[:SUBLANES, :], prev_v, taps_v)
    o_ref[0:SUBLANES, :] = (_silu(head_g) * head_v).astype(o_ref.dtype)


def _ffn_up(h, w_up, conv_w, seq):
    name = "ffn_up_conv_act"
    m, k = h.shape
    n = w_up.shape[1] // 2
    bm, bn = _block(seq, TILES[name][0]), _block(n, TILES[name][1])
    return pl.pallas_call(
        functools.partial(_ffn_up_kernel, blocks_per_seq=seq // bm),
        grid=(m // bm, n // bn),
        in_specs=[_row_spec(bm, k, 1), _col_spec(k, bn, 0), _col_spec(k, bn, n), _whole_spec(conv_w)],
        out_specs=pl.BlockSpec((bm, bn), lambda i, j: (i, j)),
        out_shape=jax.ShapeDtypeStruct((m, n), BF16),
        scratch_shapes=[pltpu.VMEM((n // bn, 2, SUBLANES, bn), F32)],
        compiler_params=_params(2),
        name=name,
    )(h, w_up, w_up, conv_w)


def _attn_kernel(lq1_ref, lk1_ref, lq2_ref, lk2_ref, g_ref, q_ref, k_ref, v_ref, o_ref,
                 m_ref, l_ref, alpha_ref, acc_ref, s_ref, p_ref, kt_ref, *, hd, lam_init):
    qi = pl.program_id(2)
    t = q_ref.shape[0]
    n_lane_tiles = t // LANES

    m_ref[...] = jnp.full(m_ref.shape, MASK_VALUE, F32)
    l_ref[...] = jnp.zeros(l_ref.shape, F32)
    acc_ref[...] = jnp.zeros(acc_ref.shape, F32)

    @pl.when(qi == 0)
    def _():
        def transpose_block(i, carry):
            rows = pl.ds(pl.multiple_of(i * t, t), t)
            for c in range(2):
                kt_ref[c, :, rows] = k_ref[rows, c * hd:(c + 1) * hd].T
            return carry
        lax.fori_loop(0, k_ref.shape[0] // t, transpose_block, 0)

    def scores(kb, slot):
        cols = pl.ds(pl.multiple_of(kb * t, t), t)
        for c in range(2):
            s_ref[slot, c] = _dot(q_ref[:, c * hd:(c + 1) * hd], kt_ref[c, :, cols])

    def softmax_rows(slot, c, r, diagonal):
        rows = pl.ds(r, SOFTMAX_ROWS)
        jd = r // LANES
        n_tiles = jd + 1 if diagonal else n_lane_tiles
        tiles = [s_ref[slot, c, rows, j * LANES:(j + 1) * LANES] for j in range(n_tiles)]
        if diagonal:
            tri = (lax.broadcasted_iota(jnp.int32, (SOFTMAX_ROWS, LANES), 1)
                   <= lax.broadcasted_iota(jnp.int32, (SOFTMAX_ROWS, LANES), 0) + r % LANES)
            tiles[jd] = jnp.where(tri, tiles[jd], MASK_VALUE)
        m_old = m_ref[c, rows, :]
        m_new = jnp.maximum(m_old, jnp.max(functools.reduce(jnp.maximum, tiles), axis=-1, keepdims=True))
        alpha = jnp.exp2(m_old - m_new)
        p_tiles = [jnp.exp2(tile - m_new) for tile in tiles]
        l_ref[c, rows, :] = alpha * l_ref[c, rows, :] + functools.reduce(jnp.add, p_tiles)
        p_tiles += [jnp.zeros((SOFTMAX_ROWS, LANES), F32)] * (n_lane_tiles - n_tiles)
        p_ref[slot, c, rows, :] = jnp.concatenate(p_tiles, axis=1).astype(BF16)
        alpha_ref[slot, c, rows, :] = alpha
        m_ref[c, rows, :] = m_new

    def accumulate(kb, slot, diagonal):
        v = v_ref[pl.ds(pl.multiple_of(kb * t, t), t), :]
        for c in range(2):
            for r in range(0, t, SOFTMAX_ROWS):
                softmax_rows(slot, c, r, diagonal)
            alpha = jnp.concatenate([alpha_ref[slot, c]] * (acc_ref.shape[-1] // LANES), axis=1)
            acc_ref[c] = alpha * acc_ref[c] + _dot(p_ref[slot, c], v)

    def pair(i, carry):
        kb = 2 * i
        scores(kb + 1, 1)
        accumulate(kb, 0, False)
        scores(kb + 2, 0)
        accumulate(kb + 1, 1, False)
        return carry

    scores(0, 0)
    lax.fori_loop(0, qi // 2, pair, 0)

    @pl.when(qi % 2 == 0)
    def _():
        accumulate(qi, 0, True)

    @pl.when(qi % 2 == 1)
    def _():
        scores(qi, 1)
        accumulate(qi - 1, 0, False)
        accumulate(qi, 1, True)

    lam = (jnp.exp(jnp.sum(lq1_ref[...] * lk1_ref[...], axis=-1, keepdims=True))
           - jnp.exp(jnp.sum(lq2_ref[...] * lk2_ref[...], axis=-1, keepdims=True))
           + lam_init)
    l1 = jnp.sum(l_ref[0], axis=-1, keepdims=True)
    l2 = jnp.sum(l_ref[1], axis=-1, keepdims=True)
    o = acc_ref[0] / l1 - lam * (acc_ref[1] / l2)
    ms = jnp.mean(o * o, axis=-1, keepdims=True)
    o = o * lax.rsqrt(ms + RMS_EPS) * g_ref[...] * (1.0 - lam_init)
    o_ref[...] = o.astype(o_ref.dtype)


def _diff_attention(qkv, lq1, lk1, lq2, lk2, subln_g, *, batch, seq, heads, hd, lam_init):
    tq = _block(seq, ATTN_BLOCK)
    assert tq % LANES == 0, "query blocks are processed in LANES-row chunks"
    nq = seq // tq
    width = 2 * hd
    vec = pl.BlockSpec((1, hd), lambda b, h, i: (0, 0))
    return pl.pallas_call(
        functools.partial(_attn_kernel, hd=hd, lam_init=lam_init),
        grid=(batch, heads, nq),
        in_specs=[
            vec, vec, vec, vec,
            pl.BlockSpec((1, width), lambda b, h, i: (0, 0)),
            pl.BlockSpec((tq, width), lambda b, h, i: (b * nq + i, h)),
            pl.BlockSpec((seq, width), lambda b, h, i: (b, heads + h)),
            pl.BlockSpec((seq, width), lambda b, h, i: (b, 2 * heads + h)),
        ],
        out_specs=pl.BlockSpec((tq, width), lambda b, h, i: (b * nq + i, h)),
        out_shape=jax.ShapeDtypeStruct((batch * seq, heads * width), BF16),
        scratch_shapes=[
            pltpu.VMEM((2, tq, LANES), F32),
            pltpu.VMEM((2, tq, LANES), F32),
            pltpu.VMEM((2, 2, tq, LANES), F32),
            pltpu.VMEM((2, tq, width), F32),
            pltpu.VMEM((2, 2, tq, tq), F32),
            pltpu.VMEM((2, 2, tq, tq), BF16),
            pltpu.VMEM((2, hd, seq), BF16),
        ],
        compiler_params=_params(3),
        name="diff_attention",
    )(lq1, lk1, lq2, lk2, subln_g, qkv, qkv, qkv)


def _merge_kernel(x_ref, a_ref, c_ref, wga_ref, wgc_ref, wa_ref, wc_ref, o_ref):
    x = x_ref[...]
    gate_a = jax.nn.sigmoid(_dot(x, wga_ref[...]))
    gate_c = jax.nn.sigmoid(_dot(x, wgc_ref[...]))
    y_a = _dot(a_ref[...], wa_ref[...])
    y_c = _dot(c_ref[...], wc_ref[...])
    o_ref[...] = (gate_a * y_a + gate_c * y_c).astype(o_ref.dtype)


def _merge(x, attn, yc, w, gate_offsets, wa, wc):
    name = "gated_merge"
    m, k = x.shape
    ka, kc = attn.shape[1], yc.shape[1]
    n = wa.shape[1]
    bm, bn = _block(m, TILES[name][0]), _block(n, TILES[name][1])
    return pl.pallas_call(
        _merge_kernel,
        grid=(m // bm, n // bn),
        in_specs=[
            _row_spec(bm, k), _row_spec(bm, ka), _row_spec(bm, kc),
            _col_spec(k, bn, gate_offsets[0]), _col_spec(k, bn, gate_offsets[1]),
            _col_spec(ka, bn, 0), _col_spec(kc, bn, 0),
        ],
        out_specs=pl.BlockSpec((bm, bn), lambda i, j: (i, j)),
        out_shape=jax.ShapeDtypeStruct((m, n), BF16),
        compiler_params=_params(2),
        name=name,
    )(x, attn, yc, w, w, wa, wc)


def _row_stats(r):
    mu = jnp.mean(r, axis=-1, keepdims=True)
    rc = r - mu
    var = jnp.mean(rc * rc, axis=-1, keepdims=True)
    return mu, lax.rsqrt(var + LN_EPS)


def _layer_norm_kernel(r_ref, g_ref, b_ref, o_ref):
    r = r_ref[...]
    mu, rstd = _row_stats(r)
    o_ref[...] = _normalized(r, mu, rstd, g_ref[...], b_ref[...]).astype(o_ref.dtype)


def _layer_norm_stats_kernel(r_ref, g_ref, b_ref, o_ref, mu_ref, rstd_ref):
    r = r_ref[...]
    mu, rstd = _row_stats(r)
    o_ref[...] = _normalized(r, mu, rstd, g_ref[...], b_ref[...]).astype(o_ref.dtype)
    mu_ref[...] = mu
    rstd_ref[...] = rstd


def _layer_norm(r, g, b, out_dtype, name, with_stats=False):
    m, d = r.shape
    bm = _block(m, LN_ROWS)
    row = pl.BlockSpec((bm, d), lambda i: (i, 0))
    vec = pl.BlockSpec((1, d), lambda i: (0, 0))
    stat = pl.BlockSpec((bm, 1), lambda i: (i, 0))
    out = jax.ShapeDtypeStruct((m, d), out_dtype)
    stat_shape = jax.ShapeDtypeStruct((m, 1), F32)
    return pl.pallas_call(
        _layer_norm_stats_kernel if with_stats else _layer_norm_kernel,
        grid=(m // bm,),
        in_specs=[row, vec, vec],
        out_specs=[row, stat, stat] if with_stats else row,
        out_shape=[out, stat_shape, stat_shape] if with_stats else out,
        compiler_params=_params(1),
        name=name,
    )(r, g, b)


def _lambda_init(layer_idx):
    return 0.8 - 0.6 * math.exp(-0.3 * layer_idx)


def kernel(x, w_in, lambda_q1, lambda_k1, lambda_q2, lambda_k2, subln_g, conv_mix_w, w_attn_out,
           w_conv_out, w_o, ln1_g, ln1_b, w_up, ffn_conv_w, w_down, ln2_g, ln2_b):
    batch, seq, d_model = x.shape
    depth = w_in.shape[0]
    hd = lambda_q1.shape[-1]
    v_width = w_attn_out.shape[1]
    heads = v_width // (2 * hd)
    qk_width = heads * 2 * hd
    conv_width = conv_mix_w.shape[-1]
    alpha = (2.0 * depth) ** 0.25
    rows = batch * seq

    o_u = 2 * qk_width + v_width
    o_b = o_u + conv_width
    o_c = o_b + conv_width
    o_ga = o_c + conv_width
    o_gc = o_ga + d_model

    q_scale = LOG2E / math.sqrt(hd)
    col_scale = jnp.concatenate(
        [jnp.full((1, qk_width), q_scale, F32), jnp.ones((1, o_u - qk_width), F32)], axis=1)

    h = x.reshape(rows, d_model)
    for l in range(depth):
        lam_init = _lambda_init(l)
        h_bf = h.astype(BF16)
        w_in_bf = w_in[l].astype(BF16)
        g1, b1 = ln1_g[l][None], ln1_b[l][None]

        qkv = _proj_scale(h_bf, w_in_bf, col_scale, "qkv_proj")
        attn = _diff_attention(
            qkv, lambda_q1[l][None], lambda_k1[l][None], lambda_q2[l][None], lambda_k2[l][None],
            subln_g[l][None], batch=batch, seq=seq, heads=heads, hd=hd, lam_init=lam_init)
        yc = _shortconv(h_bf, w_in_bf, (o_u, o_b, o_c), conv_mix_w[l], seq)
        merged = _merge(h_bf, attn, yc, w_in_bf, (o_ga, o_gc),
                        w_attn_out[l].astype(BF16), w_conv_out[l].astype(BF16))

        r1 = _proj_residual(merged, w_o[l].astype(BF16), h, alpha, "wo_proj")
        h1_bf, mu1, rstd1 = _layer_norm(r1, g1, b1, BF16, "ln1", with_stats=True)

        act = _ffn_up(h1_bf, w_up[l], ffn_conv_w[l], seq)
        r2 = _proj_ln_residual(act, w_down[l].astype(BF16), r1, mu1, rstd1, g1, b1, alpha, "down_proj")
        h = _layer_norm(r2, ln2_g[l][None], ln2_b[l][None], F32, "ln2")
    return h.reshape(batch, seq, d_model)
```
